```python
import jax, jax.numpy as jnp
from jax import lax
import numpy as np

D_MODEL = 2048
BATCH = 2
SEQ = 4096
DEPTH = 4
DEC_BATCH = 128
DEC_SEQ = 1
PAST_LEN = 8192
PAGE_SIZE = 128

MLA_HEADS = 8
QK_NOPE = 128
QK_ROPE = 64
V_HEAD = 128
Q_RANK = 512
KV_RANK = 256
ROPE_THETA = 10000.0
Q_BLOCK = 128
MLA_WIDTH = MLA_HEADS * V_HEAD
MLA_IN = Q_RANK + KV_RANK + QK_ROPE
SM_SCALE = (QK_NOPE + QK_ROPE) ** -0.5
GM_GROUPS = 4
GM_CHUNK = 128
GM_GROUP_W = 128
GM_WIDTH = GM_GROUPS * GM_GROUP_W
R_HEADS = 8
R_HEAD_DIM = 64
R_WIDTH = R_HEADS * R_HEAD_DIM
D_DECAY = 64
D_AAA = 64
D_GATE = 128
R_SHIFT_W = 3 * R_WIDTH + D_DECAY + D_AAA + D_GATE
GN_EPS = 64e-5
MIX_WIDTH = MLA_WIDTH + GM_WIDTH + R_WIDTH
IN_WIDTH = MLA_IN + 2 * GM_WIDTH + R_SHIFT_W
D_FF = -(-8 * D_MODEL // (3 * 256)) * 256
ALPHA = (2 * DEPTH) ** 0.25
BETA = (8 * DEPTH) ** -0.25

kernel_name = 'hybrid_mla_gmlp_rwkv7_deepnorm_step'


def _layer_norm(x, g, b, eps=1e-5):
    xf = x.astype(jnp.float32)
    xc = xf - jnp.mean(xf, -1, keepdims=True)
    var = jnp.mean(xc * xc, -1, keepdims=True)
    return (xc * lax.rsqrt(var + eps) * g.astype(jnp.float32) + b.astype(jnp.float32)).astype(x.dtype)


def _rms_norm(x, g, eps=1e-6):
    xf = x.astype(jnp.float32)
    return (xf * lax.rsqrt(jnp.mean(xf * xf, -1, keepdims=True) + eps) * g.astype(jnp.float32)).astype(x.dtype)


def _rope(x, pos):
    half = QK_ROPE // 2
    inv = ROPE_THETA ** (-jnp.arange(half, dtype=jnp.float32) / half)
    ang = pos.astype(jnp.float32)[:, None] * inv[None, :]
    cos = jnp.cos(ang)[None, :, None, :]
    sin = jnp.sin(ang)[None, :, None, :]
    xf = x.astype(jnp.float32)
    x1, x2 = xf[..., :half], xf[..., half:]
    return jnp.concatenate([x1 * cos - x2 * sin, x2 * cos + x1 * sin], -1).astype(x.dtype)


def _mla_attend(q_lat, q_rope, c_keys, kr_keys, q_pos, k_pos):
    s = jnp.einsum('bqhr,bkr->bhqk', q_lat, c_keys, preferred_element_type=jnp.float32)
    s = s + jnp.einsum('bqhr,bkr->bhqk', q_rope, kr_keys, preferred_element_type=jnp.float32)
    causal = (k_pos[None, :] <= q_pos[:, None])[None, None]
    s = jnp.where(causal, s * SM_SCALE, -jnp.inf)
    p = jax.nn.softmax(s, axis=-1).astype(c_keys.dtype)
    return jnp.einsum('bhqk,bkr->bqhr', p, c_keys)


def _mla_prompt_attend(q_lat, q_rope, ckv, kr):
    B, T = q_lat.shape[0], q_lat.shape[1]
    nb = T // Q_BLOCK
    pos = jnp.arange(T, dtype=jnp.int32)

    def to_blocks(t):
        return jnp.moveaxis(t.reshape(B, nb, Q_BLOCK, *t.shape[2:]), 1, 0)

    def one_block(args):
        ql, qr, qp = args
        return _mla_attend(ql, qr, ckv, kr, qp, pos)

    o = lax.map(one_block, (to_blocks(q_lat), to_blocks(q_rope), pos.reshape(nb, Q_BLOCK)))
    return jnp.moveaxis(o, 0, 1).reshape(B, T, MLA_HEADS, KV_RANK)


def _gmlp_mix(p_gm, lw):
    z = jax.nn.gelu(p_gm)
    u, v = z[..., :GM_WIDTH], z[..., GM_WIDTH:]
    v = _layer_norm(v, lw['gm_ln_g'], lw['gm_ln_b'])
    B, T, _ = v.shape
    n_chunks = -(-T // GM_CHUNK)
    vp = jnp.pad(v, ((0, 0), (0, n_chunks * GM_CHUNK - T), (0, 0)))
    vp = vp.reshape(B, n_chunks, GM_CHUNK, GM_GROUPS, GM_GROUP_W)
    causal = jnp.tril(jnp.ones((GM_CHUNK, GM_CHUNK), dtype=bool))
    w = jnp.where(causal[None], lw['gm_w_s'], jnp.zeros((), lw['gm_w_s'].dtype))
    mixed = jnp.einsum('gts,bnsgc->bntgc', w, vp) + jnp.swapaxes(lw['gm_b_s'], 0, 1)[None, None, :, :, None]
    mixed = mixed.reshape(B, n_chunks * GM_CHUNK, GM_WIDTH)[:, :T]
    return u * mixed, v


def _wkv_scan(r, decay, k, v, kk, a, S0):
    def step(S, inp):
        r_t, w_t, k_t, v_t, kk_t, a_t = inp
        sa = jnp.einsum('bhij,bhj->bhi', S, -kk_t)
        S = (S * w_t[:, :, None, :] + sa[..., :, None] * (kk_t * a_t)[:, :, None, :]
             + v_t[..., :, None] * k_t[:, :, None, :])
        return S, jnp.einsum('bhij,bhj->bhi', S, r_t)
    xs = tuple(jnp.moveaxis(t.astype(jnp.float32), 1, 0) for t in (r, decay, k, v, kk, a))
    S, ys = lax.scan(step, S0.astype(jnp.float32), xs)
    return jnp.moveaxis(ys, 0, 1), S


def _rwkv_mix(p, shift0, S0, lw):
    B, T, _ = p.shape
    prev = jnp.concatenate([shift0[:, None, :].astype(p.dtype), p[:, :-1]], axis=1)
    xm = p + (prev - p) * lw['rw_mu']
    r = xm[..., :R_WIDTH]
    k = xm[..., R_WIDTH:2 * R_WIDTH]
    v = xm[..., 2 * R_WIDTH:3 * R_WIDTH]
    o = 3 * R_WIDTH
    wd = xm[..., o:o + D_DECAY]
    ad = xm[..., o + D_DECAY:o + D_DECAY + D_AAA]
    gd = xm[..., o + D_DECAY + D_AAA:]
    w_log = -jax.nn.softplus(-(lw['rw_w0'] + jnp.tanh(wd) @ lw['rw_w2']).astype(jnp.float32)) - 0.5
    decay = jnp.exp(-jnp.exp(w_log))
    a = jax.nn.sigmoid(lw['rw_a0'] + ad @ lw['rw_a2'])
    g = jax.nn.sigmoid(gd) @ lw['rw_g2']

    def heads(t):
        return t.reshape(B, T, R_HEADS, R_HEAD_DIM)

    kk = heads(k * lw['rw_k_k']).astype(jnp.float32)
    kk = kk * lax.rsqrt(jnp.maximum(jnp.sum(kk * kk, -1, keepdims=True), 1e-24))
    k = k * (1.0 + (a - 1.0) * lw['rw_k_a'])
    r_h, k_h, v_h, a_h = heads(r), heads(k), heads(v), heads(a)
    y, S = _wkv_scan(r_h, heads(decay), k_h, v_h, kk, a_h, S0)
    y = _layer_norm(y, lw['rw_ln_g'].reshape(R_HEADS, R_HEAD_DIM),
                    lw['rw_ln_b'].reshape(R_HEADS, R_HEAD_DIM), eps=GN_EPS)
    bonus = jnp.sum(r_h * k_h * lw['rw_r_k'], -1, keepdims=True) * v_h
    y = (y + bonus.astype(jnp.float32)).astype(p.dtype).reshape(B, T, R_WIDTH) * g
    return y, p[:, -1], S


def _mixer(x, pos, past_c, past_kr, shift0, S0, lw):
    B, T, _ = x.shape
    p = jnp.einsum('btd,de->bte', x, lw['w_in'])
    p_mla = p[..., :MLA_IN]
    p_gm = p[..., MLA_IN:MLA_IN + 2 * GM_WIDTH]
    p_rw = p[..., MLA_IN + 2 * GM_WIDTH:]
    cq = _rms_norm(p_mla[..., :Q_RANK], lw['q_norm_g'])
    ckv = _rms_norm(p_mla[..., Q_RANK:Q_RANK + KV_RANK], lw['kv_norm_g'])
    kr = _rope(p_mla[..., Q_RANK + KV_RANK:][:, :, None, :], pos)[:, :, 0, :]
    q = jnp.einsum('btr,rhd->bthd', cq, lw['w_uq'])
    q_nope, q_rope = q[..., :QK_NOPE], _rope(q[..., QK_NOPE:], pos)
    q_lat = jnp.einsum('bthd,hrd->bthr', q_nope, lw['w_uk'])
    if past_c is None:
        o_lat = _mla_prompt_attend(q_lat, q_rope, ckv, kr)
    else:
        keys_c = jnp.concatenate([past_c.astype(ckv.dtype), ckv], axis=1)
        keys_r = jnp.concatenate([past_kr.astype(kr.dtype), kr], axis=1)
        k_pos = jnp.arange(keys_c.shape[1], dtype=jnp.int32)
        o_lat = _mla_attend(q_lat, q_rope, keys_c, keys_r, pos, k_pos)
    y_mla = jnp.einsum('bthr,hrd->bthd', o_lat, lw['w_uv']).reshape(B, T, MLA_WIDTH)
    y_gm, v_gm = _gmlp_mix(p_gm, lw)
    y_rw, shift_new, S_new = _rwkv_mix(p_rw, shift0, S0, lw)
    y = jnp.concatenate([y_mla, y_gm, y_rw.astype(y_mla.dtype)], axis=-1)
    h = jnp.einsum('bte,ed->btd', y, lw['w_out'])
    return h, ckv, kr, v_gm, shift_new, S_new


def _swiglu(x, w_ffn_in, w_ffn_out):
    gu = jnp.einsum('btd,df->btf', x, w_ffn_in)
    return jnp.einsum('btf,fd->btd', jax.nn.silu(gu[..., :D_FF]) * gu[..., D_FF:], w_ffn_out)


def setup_inputs(seed: int = 0) -> dict:
    key = jax.random.key(seed)
    ks = jax.random.split(key, 40)
    f32 = jnp.float32
    L = DEPTH

    def nrm(i, shape, scale=1.0):
        return scale * jax.random.normal(ks[i], shape, f32)

    def gain(i, shape):
        return 1.0 + nrm(i, shape, 0.02)

    n_pages = PAST_LEN // PAGE_SIZE
    n_used = DEC_BATCH * n_pages
    n_pool = n_used + n_used // 4
    page_table = jax.random.permutation(ks[0], n_pool)[:n_used].reshape(DEC_BATCH, n_pages).astype(jnp.int32)
    return {
        'x_prompt': nrm(1, (BATCH, SEQ, D_MODEL)),
        'x_sample': nrm(2, (DEC_BATCH, DEC_SEQ, D_MODEL)),
        'cache_c_kv': nrm(3, (L, n_pool, PAGE_SIZE, KV_RANK)),
        'cache_k_rope': nrm(4, (L, n_pool, PAGE_SIZE, QK_ROPE)),
        'state_rwkv_wkv': nrm(5, (L, DEC_BATCH, R_HEADS, R_HEAD_DIM, R_HEAD_DIM), 0.3),
        'state_rwkv_shift': nrm(6, (L, DEC_BATCH, R_SHIFT_W)),
        'page_table': page_table,
        'w_in': nrm(7, (L, D_MODEL, IN_WIDTH), D_MODEL ** -0.5),
        'q_norm_g': gain(8, (L, Q_RANK)),
        'kv_norm_g': gain(9, (L, KV_RANK)),
        'w_uq': nrm(10, (L, Q_RANK, MLA_HEADS, QK_NOPE + QK_ROPE), Q_RANK ** -0.5),
        'w_uk': nrm(11, (L, MLA_HEADS, KV_RANK, QK_NOPE), KV_RANK ** -0.5),
        'w_uv': nrm(12, (L, MLA_HEADS, KV_RANK, V_HEAD), KV_RANK ** -0.5),
        'gm_ln_g': gain(13, (L, GM_WIDTH)),
        'gm_ln_b': nrm(14, (L, GM_WIDTH), 0.02),
        'gm_w_s': nrm(15, (L, GM_GROUPS, GM_CHUNK, GM_CHUNK), 0.5 * GM_CHUNK ** -0.5),
        'gm_b_s': gain(16, (L, GM_GROUPS, GM_CHUNK)),
        'rw_mu': jax.random.uniform(ks[17], (L, R_SHIFT_W), f32, 0.0, 1.0),
        'rw_w0': jax.random.uniform(ks[18], (L, R_WIDTH), f32, -6.0, 0.0),
        'rw_w2': nrm(19, (L, D_DECAY, R_WIDTH), 0.5 * D_DECAY ** -0.5),
        'rw_a0': nrm(20, (L, R_WIDTH), 0.1),
        'rw_a2': nrm(21, (L, D_AAA, R_WIDTH), 0.5 * D_AAA ** -0.5),
        'rw_g2': nrm(22, (L, D_GATE, R_WIDTH), D_GATE ** -0.5),
        'rw_k_k': 0.85 + nrm(23, (L, R_WIDTH), 0.02),
        'rw_k_a': gain(24, (L, R_WIDTH)),
        'rw_r_k': nrm(25, (L, R_HEADS, R_HEAD_DIM), 0.1),
        'rw_ln_g': gain(26, (L, R_WIDTH)),
        'rw_ln_b': nrm(27, (L, R_WIDTH), 0.02),
        'w_out': nrm(28, (L, MIX_WIDTH, D_MODEL), BETA * MIX_WIDTH ** -0.5),
        'ln1_g': gain(29, (L, D_MODEL)),
        'ln1_b': nrm(30, (L, D_MODEL), 0.02),
        'w_ffn_in': nrm(31, (L, D_MODEL, 2 * D_FF), D_MODEL ** -0.5),
        'w_ffn_out': nrm(32, (L, D_FF, D_MODEL), BETA * D_FF ** -0.5),
        'ln2_g': gain(33, (L, D_MODEL)),
        'ln2_b': nrm(34, (L, D_MODEL), 0.02),
    }


def reference(x_prompt, x_sample, cache_c_kv, cache_k_rope, state_rwkv_wkv, state_rwkv_shift,
              page_table, w_in, q_norm_g, kv_norm_g, w_uq, w_uk, w_uv, gm_ln_g, gm_ln_b,
              gm_w_s, gm_b_s, rw_mu, rw_w0, rw_w2, rw_a0, rw_a2, rw_g2, rw_k_k, rw_k_a,
              rw_r_k, rw_ln_g, rw_ln_b, w_out, ln1_g, ln1_b, w_ffn_in, w_ffn_out, ln2_g, ln2_b):
    n_prompt, t_prompt, _ = x_prompt.shape
    n_dec, t_dec, _ = x_sample.shape
    past_len = page_table.shape[1] * PAGE_SIZE
    pos_p = jnp.arange(t_prompt, dtype=jnp.int32)
    pos_s = past_len + jnp.arange(t_dec, dtype=jnp.int32)
    shift_zero = jnp.zeros((n_prompt, R_SHIFT_W), x_prompt.dtype)
    wkv_zero = jnp.zeros((n_prompt, R_HEADS, R_HEAD_DIM, R_HEAD_DIM), jnp.float32)
    xp, xs = x_prompt, x_sample
    ckv_p_l, kr_p_l, ckv_s_l, kr_s_l = [], [], [], []
    wkv_p_l, sh_p_l, wkv_s_l, sh_s_l, vg_s_l = [], [], [], [], []
    for l in range(DEPTH):
        lw = {
            'w_in': w_in[l], 'q_norm_g': q_norm_g[l], 'kv_norm_g': kv_norm_g[l],
            'w_uq': w_uq[l], 'w_uk': w_uk[l], 'w_uv': w_uv[l],
            'gm_ln_g': gm_ln_g[l], 'gm_ln_b': gm_ln_b[l], 'gm_w_s': gm_w_s[l], 'gm_b_s': gm_b_s[l],
            'rw_mu': rw_mu[l], 'rw_w0': rw_w0[l], 'rw_w2': rw_w2[l], 'rw_a0': rw_a0[l],
            'rw_a2': rw_a2[l], 'rw_g2': rw_g2[l], 'rw_k_k': rw_k_k[l], 'rw_k_a': rw_k_a[l],
            'rw_r_k': rw_r_k[l], 'rw_ln_g': rw_ln_g[l], 'rw_ln_b': rw_ln_b[l], 'w_out': w_out[l],
        }
        hp, ckv_p, kr_p, _, sh_p, S_p = _mixer(xp, pos_p, None, None, shift_zero, wkv_zero, lw)
        xp = _layer_norm(ALPHA * xp + hp, ln1_g[l], ln1_b[l])
        xp = _layer_norm(ALPHA * xp + _swiglu(xp, w_ffn_in[l], w_ffn_out[l]), ln2_g[l], ln2_b[l])
        past_c = cache_c_kv[l][page_table].reshape(n_dec, past_len, KV_RANK)
        past_kr = cache_k_rope[l][page_table].reshape(n_dec, past_len, QK_ROPE)
        hs, ckv_s, kr_s, vg_s, sh_s, S_s = _mixer(xs, pos_s, past_c, past_kr,
                                                  state_rwkv_shift[l], state_rwkv_wkv[l], lw)
        xs = _layer_norm(ALPHA * xs + hs, ln1_g[l], ln1_b[l])
        xs = _layer_norm(ALPHA * xs + _swiglu(xs, w_ffn_in[l], w_ffn_out[l]), ln2_g[l], ln2_b[l])
        ckv_p_l.append(ckv_p)
        kr_p_l.append(kr_p)
        ckv_s_l.append(ckv_s)
        kr_s_l.append(kr_s)
        wkv_p_l.append(S_p.astype(x_prompt.dtype))
        sh_p_l.append(sh_p)
        wkv_s_l.append(S_s.astype(state_rwkv_wkv.dtype))
        sh_s_l.append(sh_s.astype(state_rwkv_shift.dtype))
        vg_s_l.append(vg_s)
    return (xp, xs, jnp.stack(ckv_p_l), jnp.stack(kr_p_l), jnp.stack(ckv_s_l), jnp.stack(kr_s_l),
            jnp.stack(wkv_p_l), jnp.stack(sh_p_l), jnp.stack(wkv_s_l), jnp.stack(sh_s_l),
            jnp.stack(vg_s_l))
```

```python
import functools

import jax
import jax.numpy as jnp
from jax import lax
from jax.experimental import pallas as pl
from jax.experimental.pallas import tpu as pltpu

F32 = jnp.float32
BF16 = jnp.bfloat16
HIGHEST = lax.Precision.HIGHEST

D_MODEL = 2048
DEPTH = 4
PAGE_SIZE = 128
MLA_HEADS = 8
QK_NOPE = 128
QK_ROPE = 64
V_HEAD = 128
Q_RANK = 512
KV_RANK = 256
ROPE_THETA = 10000.0
MLA_WIDTH = MLA_HEADS * V_HEAD
SM_SCALE = (QK_NOPE + QK_ROPE) ** -0.5
GM_GROUPS = 4
GM_CHUNK = 128
GM_GROUP_W = 128
GM_WIDTH = GM_GROUPS * GM_GROUP_W
R_HEADS = 8
R_HEAD_DIM = 64
R_WIDTH = R_HEADS * R_HEAD_DIM
D_DECAY = 64
D_AAA = 64
D_GATE = 128
R_SHIFT_W = 3 * R_WIDTH + D_DECAY + D_AAA + D_GATE
GN_EPS = 64e-5
D_FF = -(-8 * D_MODEL // (3 * 256)) * 256
ALPHA = (2 * DEPTH) ** 0.25
MLA_IN_PAD = 1024
WKV_CHUNK = 64
V7X_VMEM_LIMIT = 56 * 1024 * 1024


def _cparams(sem):
    return pltpu.CompilerParams(dimension_semantics=sem, vmem_limit_bytes=V7X_VMEM_LIMIT)


def _tile(n, pref):
    t = min(n, pref)
    while n % t:
        t //= 2
    return t


def _nt(a, b):
    return lax.dot_general(a, b, (((1,), (1,)), ((), ())), preferred_element_type=F32)


def _split3(x):
    hi = x.astype(BF16)
    r1 = x - hi.astype(F32)
    mid = r1.astype(BF16)
    lo = (r1 - mid.astype(F32)).astype(BF16)
    return hi, mid, lo


def _sel_dot(x, e, nt=False):
    hi, mid, lo = _split3(x)
    d = _nt if nt else functools.partial(jnp.dot, preferred_element_type=F32)
    return (d(hi, e) + d(mid, e)) + d(lo, e)


def _layer_norm_rows(x, g, b, eps):
    xc = x - jnp.mean(x, -1, keepdims=True)
    var = jnp.mean(xc * xc, -1, keepdims=True)
    return xc * lax.rsqrt(var + eps) * g + b


def _mla_in_kernel(x_ref, w_ref, qg_ref, kvg_ref, cos64_ref, sin64_ref, cos512_ref, sin512_ref,
                   wqn_ref, wqr_ref, wqrr_ref, wuk_ref,
                   ckv_ref, kr_ref, ckvb_ref, krb_ref, qlat_ref, qrope_ref):
    x = x_ref[0].astype(BF16)
    p = jnp.dot(x, w_ref[...], preferred_element_type=F32)
    cq = p[:, :Q_RANK]
    cq = cq * lax.rsqrt(jnp.mean(cq * cq, -1, keepdims=True) + 1e-6) * qg_ref[...]
    ckv = p[:, Q_RANK:Q_RANK + KV_RANK]
    ckv = ckv * lax.rsqrt(jnp.mean(ckv * ckv, -1, keepdims=True) + 1e-6) * kvg_ref[...]
    kr = p[:, 768:832] * cos64_ref[0] + p[:, 896:960] * sin64_ref[0]
    ckv_ref[0] = ckv
    kr_ref[0] = kr
    ckvb_ref[0] = ckv.astype(BF16)
    krb_ref[0] = kr.astype(BF16)
    cqb = cq.astype(BF16)
    qn = jnp.dot(cqb, wqn_ref[...], preferred_element_type=F32)
    qr = jnp.dot(cqb, wqr_ref[...], preferred_element_type=F32)
    qrr = jnp.dot(cqb, wqrr_ref[...], preferred_element_type=F32)
    qrope = qr * cos512_ref[0] + qrr * sin512_ref[0]
    for h in range(MLA_HEADS):
        ql = jnp.dot(qn[:, h * QK_NOPE:(h + 1) * QK_NOPE].astype(BF16), wuk_ref[h],
                     preferred_element_type=F32)
        qlat_ref[0, h] = ql.astype(BF16)
        qrope_ref[0, h] = qrope[:, h * QK_ROPE:(h + 1) * QK_ROPE].astype(BF16)


def _mla_in(x, tabs, lw):
    B, T, D = x.shape
    tm = _tile(T, 512)
    cos64, sin64, cos512, sin512 = tabs
    row = lambda w: pl.BlockSpec((1, tm, w), lambda b, t: (b, t, 0))
    tab = lambda w: pl.BlockSpec((1, tm, w), lambda b, t: (0, t, 0))
    full = lambda a: pl.BlockSpec(a.shape, lambda b, t: (0,) * a.ndim)
    hd = lambda w: pl.BlockSpec((1, MLA_HEADS, tm, w), lambda b, t: (b, 0, t, 0))
    ws = (lw['w_mla'], lw['q_norm_g'], lw['kv_norm_g'])
    ws2 = (lw['w_uq_nope'], lw['w_uq_rope'], lw['w_uq_rope_rot'], lw['w_uk_t'])
    return pl.pallas_call(
        _mla_in_kernel,
        grid=(B, T // tm),
        in_specs=[row(D)] + [full(a) for a in ws] + [tab(64), tab(64), tab(512), tab(512)] + [full(a) for a in ws2],
        out_specs=[row(KV_RANK), row(QK_ROPE), row(KV_RANK), row(QK_ROPE), hd(KV_RANK), hd(QK_ROPE)],
        out_shape=[jax.ShapeDtypeStruct((B, T, KV_RANK), F32), jax.ShapeDtypeStruct((B, T, QK_ROPE), F32),
                   jax.ShapeDtypeStruct((B, T, KV_RANK), BF16), jax.ShapeDtypeStruct((B, T, QK_ROPE), BF16),
                   jax.ShapeDtypeStruct((B, MLA_HEADS, T, KV_RANK), BF16),
                   jax.ShapeDtypeStruct((B, MLA_HEADS, T, QK_ROPE), BF16)],
        compiler_params=_cparams(("parallel", "parallel")),
        name="mla_in",
    )(x, *ws, cos64, sin64, cos512, sin512, *ws2)


def _attn_kernel(ql_ref, qr_ref, kc_ref, kr_ref, wuv_ref, o_ref, m_sc, l_sc, acc_sc, *, tq, tk):
    qi = pl.program_id(1)
    kj = pl.program_id(2)
    H = MLA_HEADS

    @pl.when(kj == 0)
    def _():
        m_sc[...] = jnp.full(m_sc.shape, -jnp.inf, F32)
        l_sc[...] = jnp.zeros(l_sc.shape, F32)
        acc_sc[...] = jnp.zeros(acc_sc.shape, F32)

    @pl.when(kj * tk <= qi * tq + tq - 1)
    def _():
        ql = ql_ref[0].reshape(H * tq, KV_RANK)
        qr = qr_ref[0].reshape(H * tq, QK_ROPE)
        kc = kc_ref[0]
        s = (_nt(ql, kc) + _nt(qr, kr_ref[0])) * SM_SCALE
        q_pos = qi * tq + lax.broadcasted_iota(jnp.int32, (tq, tk), 0)
        k_pos = kj * tk + lax.broadcasted_iota(jnp.int32, (tq, tk), 1)
        s = jnp.where((k_pos <= q_pos)[None], s.reshape(H, tq, tk), -jnp.inf).reshape(H * tq, tk)
        m_old = m_sc[...]
        m_new = jnp.maximum(m_old, jnp.max(s, -1, keepdims=True))
        scale = jnp.exp(m_old - m_new)
        p = jnp.exp(s - m_new)
        l_sc[...] = scale * l_sc[...] + jnp.sum(p, -1, keepdims=True)
        acc_sc[...] = scale * acc_sc[...] + jnp.dot(p.astype(BF16), kc, preferred_element_type=F32)
        m_sc[...] = m_new

    @pl.when(kj == pl.num_programs(2) - 1)
    def _():
        o = (acc_sc[...] / l_sc[...]).astype(BF16)
        for h in range(H):
            o_ref[0, :, h * V_HEAD:(h + 1) * V_HEAD] = jnp.dot(
                o[h * tq:(h + 1) * tq], wuv_ref[h], preferred_element_type=F32).astype(o_ref.dtype)


def _attn_prompt(qlat, qrope, ckvb, krb, w_uv):
    B, H, T, _ = qlat.shape
    tq = _tile(T, 128)
    tk = _tile(T, 512)
    last = lambda i: (i * tq + tq - 1) // tk
    return pl.pallas_call(
        functools.partial(_attn_kernel, tq=tq, tk=tk),
        grid=(B, T // tq, T // tk),
        in_specs=[pl.BlockSpec((1, H, tq, KV_RANK), lambda b, i, j: (b, 0, i, 0)),
                  pl.BlockSpec((1, H, tq, QK_ROPE), lambda b, i, j: (b, 0, i, 0)),
                  pl.BlockSpec((1, tk, KV_RANK), lambda b, i, j: (b, jnp.minimum(j, last(i)), 0)),
                  pl.BlockSpec((1, tk, QK_ROPE), lambda b, i, j: (b, jnp.minimum(j, last(i)), 0)),
                  pl.BlockSpec(w_uv.shape, lambda b, i, j: (0, 0, 0))],
        out_specs=pl.BlockSpec((1, tq, MLA_WIDTH), lambda b, i, j: (b, i, 0)),
        out_shape=jax.ShapeDtypeStruct((B, T, MLA_WIDTH), BF16),
        scratch_shapes=[pltpu.VMEM((H * tq, 1), F32), pltpu.VMEM((H * tq, 1), F32),
                        pltpu.VMEM((H * tq, KV_RANK), F32)],
        compiler_params=_cparams(("parallel", "parallel", "arbitrary")),
        name="attn_prompt",
    )(qlat, qrope, ckvb, krb, w_uv)


def _attn_sample_kernel(pt_ref, ql_ref, qr_ref, cnew_ref, rnew_ref, wuv_ref, cc_hbm, cr_hbm, o_ref,
                        cbuf, rbuf, sem, s_sc, *, layer, n_pages):
    b = pl.program_id(0)
    nb = pl.num_programs(0)
    H = MLA_HEADS

    def copies(bb, slot):
        out = []
        for j in range(n_pages):
            pg = pt_ref[bb, j]
            out.append(pltpu.make_async_copy(cc_hbm.at[layer, pg], cbuf.at[slot, j], sem.at[0, slot]))
            out.append(pltpu.make_async_copy(cr_hbm.at[layer, pg], rbuf.at[slot, j], sem.at[1, slot]))
        return out

    @pl.when(b == 0)
    def _():
        for c in copies(0, 0):
            c.start()

    slot = lax.rem(b, 2)

    @pl.when(b + 1 < nb)
    def _():
        for c in copies(b + 1, 1 - slot):
            c.start()

    for c in copies(b, slot):
        c.wait()

    ql = ql_ref[0]
    qr = qr_ref[0]
    for j in range(n_pages):
        kc = cbuf[slot, j].astype(BF16)
        kr = rbuf[slot, j].astype(BF16)
        s_sc[:, j * PAGE_SIZE:(j + 1) * PAGE_SIZE] = (_nt(ql, kc) + _nt(qr, kr)) * SM_SCALE
    cn = cnew_ref[0].astype(BF16)
    s_new = (jnp.sum(ql.astype(F32) * cn.astype(F32), -1, keepdims=True)
             + jnp.sum(qr.astype(F32) * rnew_ref[0].astype(BF16).astype(F32), -1, keepdims=True)) * SM_SCALE
    s = s_sc[...]
    m = jnp.maximum(jnp.max(s, -1, keepdims=True), s_new)
    p = jnp.exp(s - m)
    p_new = jnp.exp(s_new - m)
    l = jnp.sum(p, -1, keepdims=True) + p_new
    p = (p / l).astype(BF16)
    acc = (p_new / l).astype(BF16).astype(F32) * cn.astype(F32)
    for j in range(n_pages):
        acc = acc + jnp.dot(p[:, j * PAGE_SIZE:(j + 1) * PAGE_SIZE], cbuf[slot, j].astype(BF16),
                            preferred_element_type=F32)
    ob = acc.astype(BF16)
    for h in range(H):
        o_ref[0, :, h * V_HEAD:(h + 1) * V_HEAD] = jnp.dot(
            ob[h:h + 1], wuv_ref[h], preferred_element_type=F32).astype(o_ref.dtype)


def _attn_sample(layer, page_table, qlat, qrope, ckv_new, kr_new, w_uv, cache_c, cache_r):
    B, n_pages = page_table.shape
    H = MLA_HEADS
    grid_spec = pltpu.PrefetchScalarGridSpec(
        num_scalar_prefetch=1,
        grid=(B,),
        in_specs=[pl.BlockSpec((1, H, KV_RANK), lambda b, pt: (b, 0, 0)),
                  pl.BlockSpec((1, H, QK_ROPE), lambda b, pt: (b, 0, 0)),
                  pl.BlockSpec((1, 1, KV_RANK), lambda b, pt: (b, 0, 0)),
                  pl.BlockSpec((1, 1, QK_ROPE), lambda b, pt: (b, 0, 0)),
                  pl.BlockSpec(w_uv.shape, lambda b, pt: (0, 0, 0)),
                  pl.BlockSpec(memory_space=pl.ANY),
                  pl.BlockSpec(memory_space=pl.ANY)],
        out_specs=pl.BlockSpec((1, 1, MLA_WIDTH), lambda b, pt: (b, 0, 0)),
        scratch_shapes=[pltpu.VMEM((2, n_pages, PAGE_SIZE, KV_RANK), F32),
                        pltpu.VMEM((2, n_pages, PAGE_SIZE, QK_ROPE), F32),
                        pltpu.SemaphoreType.DMA((2, 2)),
                        pltpu.VMEM((H, n_pages * PAGE_SIZE), F32)])
    return pl.pallas_call(
        functools.partial(_attn_sample_kernel, layer=layer, n_pages=n_pages),
        grid_spec=grid_spec,
        out_shape=jax.ShapeDtypeStruct((B, 1, MLA_WIDTH), BF16),
        compiler_params=_cparams(("arbitrary",)),
        name="attn_sample",
    )(page_table, qlat, qrope, ckv_new, kr_new, w_uv, cache_c, cache_r)


def _gelu_tanh(x):
    return 0.5 * x * (1.0 + jnp.tanh(0.7978845608028654 * (x + 0.044715 * (x * x * x))))


def _gm_kernel(x_ref, w_ref, lng_ref, lnb_ref, ws_ref, bs_ref, y_ref, *v_out, tm, single):
    x = x_ref[...].astype(BF16)
    z = _gelu_tanh(jnp.dot(x, w_ref[...], preferred_element_type=F32))
    u = z[:, :GM_WIDTH]
    v = _layer_norm_rows(z[:, GM_WIDTH:], lng_ref[...], lnb_ref[...], 1e-5)
    if single:
        y_ref[...] = (u * (v * ws_ref[...] + bs_ref[...])).astype(y_ref.dtype)
        v_out[0][...] = v
        return
    t_i = lax.broadcasted_iota(jnp.int32, (GM_CHUNK, GM_CHUNK), 0)
    s_i = lax.broadcasted_iota(jnp.int32, (GM_CHUNK, GM_CHUNK), 1)
    wm = [jnp.where(s_i <= t_i, ws_ref[g], 0.0).astype(BF16) for g in range(GM_GROUPS)]
    vb = v.astype(BF16)
    for c in range(tm // GM_CHUNK):
        rows = slice(c * GM_CHUNK, (c + 1) * GM_CHUNK)
        for g in range(GM_GROUPS):
            cols = slice(g * GM_GROUP_W, (g + 1) * GM_GROUP_W)
            mixed = jnp.dot(wm[g], vb[rows, cols], preferred_element_type=F32) + bs_ref[:, cols]
            y_ref[rows, cols] = (u[rows, cols] * mixed).astype(y_ref.dtype)


def _gmlp(x2, lw, single):
    M, D = x2.shape
    tm = _tile(M, 256)
    ws, bs = (lw['gm_w00'], lw['gm_b0']) if single else (lw['gm_w_s'], lw['gm_b_full'])
    full = lambda a: pl.BlockSpec(a.shape, lambda i: (0,) * a.ndim)
    row = lambda w: pl.BlockSpec((tm, w), lambda i: (i, 0))
    out_specs = [row(GM_WIDTH)]
    out_shape = [jax.ShapeDtypeStruct((M, GM_WIDTH), BF16)]
    if single:
        out_specs.append(row(GM_WIDTH))
        out_shape.append(jax.ShapeDtypeStruct((M, GM_WIDTH), F32))
    ins = (lw['w_gm'], lw['gm_ln_g'], lw['gm_ln_b'], ws, bs)
    return pl.pallas_call(
        functools.partial(_gm_kernel, tm=tm, single=single),
        grid=(M // tm,),
        in_specs=[row(D)] + [full(a) for a in ins],
        out_specs=out_specs, out_shape=out_shape,
        compiler_params=_cparams(("parallel",)),
        name="gmlp",
    )(x2, *ins)


def _rw_in_kernel(x_ref, prev_ref, w_ref, mu_ref, wda_ref, w0a0_ref, g2_ref, kk_ref_w, ka_ref, rk_ref, e_ref,
                  r_o, lw_o, k_o, v_o, kk_o, b_o, g_o, bonus_o, shift_o, carry, *, tm, seq):
    t = pl.program_id(1)
    x = x_ref[0].astype(BF16)
    p = jnp.dot(x, w_ref[...], preferred_element_type=F32)
    if seq:
        @pl.when(t == 0)
        def _():
            carry[...] = prev_ref[0]
        row = lax.broadcasted_iota(jnp.int32, (tm, 1), 0)
        prev = jnp.where(row == 0, carry[...], pltpu.roll(p, 1, axis=0))
        carry[...] = p[tm - 1:tm]
    else:
        prev = prev_ref[0]
    shift_o[0] = p[tm - 1:tm] if seq else p
    xm = p + (prev - p) * mu_ref[...]
    r = xm[:, :R_WIDTH]
    k = xm[:, R_WIDTH:2 * R_WIDTH]
    v = xm[:, 2 * R_WIDTH:3 * R_WIDTH]
    da = xm[:, 3 * R_WIDTH:3 * R_WIDTH + D_DECAY + D_AAA]
    lane = lax.broadcasted_iota(jnp.int32, da.shape, 1)
    da = jnp.where(lane < D_DECAY, jnp.tanh(da), da)
    m = jnp.dot(da.astype(BF16), wda_ref[...], preferred_element_type=F32) + w0a0_ref[...]
    wv = m[:, :R_WIDTH]
    w_log = -(jnp.maximum(-wv, 0.0) + jnp.log1p(jnp.exp(-jnp.abs(wv)))) - 0.5
    logdecay = -jnp.exp(w_log)
    a = jax.nn.sigmoid(m[:, R_WIDTH:])
    gd = xm[:, 3 * R_WIDTH + D_DECAY + D_AAA:]
    g = jnp.dot(jax.nn.sigmoid(gd).astype(BF16), g2_ref[...], preferred_element_type=F32)
    e = e_ref[...]
    kk = k * kk_ref_w[...]
    kk = kk * lax.rsqrt(jnp.maximum(_sel_dot(kk * kk, e), 1e-24))
    k2 = k * (1.0 + (a - 1.0) * ka_ref[...])
    bonus = _sel_dot(r * k2 * rk_ref[...], e) * v
    g_o[0] = g
    bonus_o[0] = bonus
    bb = kk * a
    for h in range(R_HEADS):
        c = slice(h * R_HEAD_DIM, (h + 1) * R_HEAD_DIM)
        r_o[0, h] = r[:, c]
        lw_o[0, h] = logdecay[:, c]
        k_o[0, h] = k2[:, c]
        v_o[0, h] = v[:, c]
        kk_o[0, h] = kk[:, c]
        b_o[0, h] = bb[:, c]


def _rw_in(x, prev, lw, seq):
    B, T, D = x.shape
    tm = _tile(T, 256)
    row = lambda w: pl.BlockSpec((1, tm, w), lambda b, t: (b, t, 0))
    full = lambda a: pl.BlockSpec(a.shape, lambda b, t: (0,) * a.ndim)
    hd = pl.BlockSpec((1, R_HEADS, tm, R_HEAD_DIM), lambda b, t: (b, 0, t, 0))
    ws = (lw['w_rw'], lw['rw_mu'], lw['rw_wda'], lw['rw_w0a0'], lw['rw_g2'], lw['rw_k_k'], lw['rw_k_a'],
          lw['rw_r_k'], lw['head_ind'])
    if seq:
        prev_spec = pl.BlockSpec((1, 1, R_SHIFT_W), lambda b, t: (b, 0, 0))
        shift_spec = pl.BlockSpec((1, 1, R_SHIFT_W), lambda b, t: (b, 0, 0))
        shift_shape = jax.ShapeDtypeStruct((B, 1, R_SHIFT_W), F32)
    else:
        prev_spec = row(R_SHIFT_W)
        shift_spec = row(R_SHIFT_W)
        shift_shape = jax.ShapeDtypeStruct((B, T, R_SHIFT_W), F32)
    hshape = jax.ShapeDtypeStruct((B, R_HEADS, T, R_HEAD_DIM), F32)
    fshape = jax.ShapeDtypeStruct((B, T, R_WIDTH), F32)
    return pl.pallas_call(
        functools.partial(_rw_in_kernel, tm=tm, seq=seq),
        grid=(B, T // tm),
        in_specs=[row(D), prev_spec] + [full(a) for a in ws],
        out_specs=[hd] * 6 + [row(R_WIDTH), row(R_WIDTH), shift_spec],
        out_shape=[hshape] * 6 + [fshape, fshape, shift_shape],
        scratch_shapes=[pltpu.VMEM((1, R_SHIFT_W), F32)],
        compiler_params=_cparams(("parallel", "arbitrary")),
        name="rw_in",
    )(x, prev, *ws)


def _hdot(a, b):
    return jnp.dot(a, b, preferred_element_type=F32, precision=HIGHEST)


def _hdot_nt(a, b):
    return lax.dot_general(a, b, (((1,), (1,)), ((), ())), preferred_element_type=F32, precision=HIGHEST)


def _hdot_tn(a, b):
    return lax.dot_general(a, b, (((0,), (0,)), ((), ())), preferred_element_type=F32, precision=HIGHEST)


def _wkv_kernel(r_ref, lw_ref, k_ref, v_ref, kk_ref, b_ref, y_ref, s_ref, state):
    c = pl.program_id(1)
    C = WKV_CHUNK

    @pl.when(c == 0)
    def _():
        state[...] = jnp.zeros(state.shape, F32)

    ti = lax.broadcasted_iota(jnp.int32, (C, C), 0)
    si = lax.broadcasted_iota(jnp.int32, (C, C), 1)
    incl = (si <= ti).astype(F32)
    strict = (si < ti).astype(F32)
    eye = (si == ti).astype(F32)
    ys = []
    for h in range(R_HEADS):
        lw = lw_ref[0, h]
        cum = _hdot(incl, lw)
        p_in = jnp.exp(cum)
        p_inv = jnp.exp(-cum)
        kk = kk_ref[0, h]
        A = -kk * jnp.exp(cum - lw)
        Bm = b_ref[0, h] * p_inv
        K = k_ref[0, h] * p_inv
        R = r_ref[0, h] * p_in
        V = v_ref[0, h]
        N = strict * _hdot_nt(A, Bm)
        Mm = strict * _hdot_nt(A, K)
        XA = A
        XU = _hdot(Mm, V)
        Np = N
        n_steps = C.bit_length() - 1
        for i in range(n_steps):
            XA = XA + _hdot(Np, XA)
            XU = XU + _hdot(Np, XU)
            if i + 1 < n_steps:
                Np = _hdot(Np, Np)
        RB = incl * _hdot_nt(R, Bm)
        RK = incl * _hdot_nt(R, K)
        Q = R + _hdot(RB, XA)
        Y0 = _hdot(RB, XU) + _hdot(RK, V)
        pc = p_in[C - 1:C]
        G = (eye + _hdot_tn(XA, Bm)) * pc
        Hm = (_hdot_tn(XU, Bm) + _hdot_tn(V, K)) * pc
        S0 = state[h]
        ys.append(_hdot_nt(Q, S0) + Y0)
        S1 = _hdot(S0, G) + Hm
        state[h] = S1
        s_ref[0, h] = S1
    y_ref[0] = jnp.concatenate(ys, axis=-1)


def _wkv_prompt(r, lw, k, v, kk, b):
    B, H, T, N = r.shape
    C = WKV_CHUNK
    hd = pl.BlockSpec((1, H, C, N), lambda bi, c: (bi, 0, c, 0))
    return pl.pallas_call(
        _wkv_kernel,
        grid=(B, T // C),
        in_specs=[hd] * 6,
        out_specs=[pl.BlockSpec((1, C, H * N), lambda bi, c: (bi, c, 0)),
                   pl.BlockSpec((1, H, N, N), lambda bi, c: (bi, 0, 0, 0))],
        out_shape=[jax.ShapeDtypeStruct((B, T, H * N), F32), jax.ShapeDtypeStruct((B, H, N, N), F32)],
        scratch_shapes=[pltpu.VMEM((H, N, N), F32)],
        compiler_params=_cparams(("parallel", "arbitrary")),
        name="wkv_prompt",
    )(r, lw, k, v, kk, b)


def _wkv_step_kernel(r_ref, lw_ref, k_ref, v_ref, kk_ref, b_ref, s_ref, et_ref, er_ref, y_ref, so_ref):
    et = et_ref[...]
    er = er_ref[...]
    S = s_ref[...]
    over_j = lambda x: _sel_dot(x, et)
    over_i = lambda x: _sel_dot(x, er)
    red_j = lambda x: _sel_dot(x, er, nt=True)
    sa = red_j(S * over_j(-kk_ref[...]))
    S1 = (S * over_j(jnp.exp(lw_ref[...])) + over_i(sa) * over_j(b_ref[...])
          + over_i(v_ref[...]) * over_j(k_ref[...]))
    so_ref[...] = S1
    y_ref[...] = red_j(S1 * over_j(r_ref[...]))


def _wkv_step(r, lw, k, v, kk, b, S, lw_consts):
    R, N = r.shape
    tr = _tile(R, 128)
    et, er = lw_consts
    row = lambda w: pl.BlockSpec((tr, w), lambda i: (i, 0))
    full = lambda a: pl.BlockSpec(a.shape, lambda i: (0,) * a.ndim)
    return pl.pallas_call(
        _wkv_step_kernel,
        grid=(R // tr,),
        in_specs=[row(N)] * 6 + [row(N * N), full(et), full(er)],
        out_specs=[row(N), row(N * N)],
        out_shape=[jax.ShapeDtypeStruct((R, N), F32), jax.ShapeDtypeStruct((R, N * N), F32)],
        compiler_params=_cparams(("parallel",)),
        name="wkv_step",
    )(r, lw, k, v, kk, b, S, et, er)


def _rw_post_kernel(y_ref, bonus_ref, g_ref, lng_ref, lnb_ref, e_ref, o_ref):
    e = e_ref[...]
    y = y_ref[...]
    mean = _sel_dot(y, e) * (1.0 / R_HEAD_DIM)
    yc = y - mean
    var = _sel_dot(yc * yc, e) * (1.0 / R_HEAD_DIM)
    yn = yc * lax.rsqrt(var + GN_EPS) * lng_ref[...] + lnb_ref[...]
    o_ref[...] = ((yn + bonus_ref[...]) * g_ref[...]).astype(o_ref.dtype)


def _rw_post(y, bonus, g, lw):
    M, W = y.shape
    tm = _tile(M, 512)
    row = pl.BlockSpec((tm, W), lambda i: (i, 0))
    full = lambda a: pl.BlockSpec(a.shape, lambda i: (0,) * a.ndim)
    ws = (lw['rw_ln_g'], lw['rw_ln_b'], lw['head_ind'])
    return pl.pallas_call(
        _rw_post_kernel,
        grid=(M // tm,),
        in_specs=[row, row, row] + [full(a) for a in ws],
        out_specs=row,
        out_shape=jax.ShapeDtypeStruct((M, W), BF16),
        compiler_params=_cparams(("parallel",)),
        name="rw_post",
    )(y, bonus, g, *ws)


def _out_ln_kernel(x_ref, ya_ref, yg_ref, yr_ref, w_ref, g_ref, b_ref, o_ref):
    h = jnp.dot(ya_ref[...], w_ref[:MLA_WIDTH], preferred_element_type=F32)
    h = h + jnp.dot(yg_ref[...], w_ref[MLA_WIDTH:MLA_WIDTH + GM_WIDTH], preferred_element_type=F32)
    h = h + jnp.dot(yr_ref[...], w_ref[MLA_WIDTH + GM_WIDTH:], preferred_element_type=F32)
    o_ref[...] = _layer_norm_rows(ALPHA * x_ref[...] + h, g_ref[...], b_ref[...], 1e-5)


def _out_ln(x2, y_mla, y_gm, y_rw, lw):
    M, D = x2.shape
    tm = _tile(M, 512)
    row = lambda w: pl.BlockSpec((tm, w), lambda i: (i, 0))
    full = lambda a: pl.BlockSpec(a.shape, lambda i: (0,) * a.ndim)
    ws = (lw['w_out'], lw['ln1_g'], lw['ln1_b'])
    return pl.pallas_call(
        _out_ln_kernel,
        grid=(M // tm,),
        in_specs=[row(D), row(MLA_WIDTH), row(GM_WIDTH), row(R_WIDTH)] + [full(a) for a in ws],
        out_specs=row(D),
        out_shape=jax.ShapeDtypeStruct((M, D), F32),
        compiler_params=_cparams(("parallel",)),
        name="out_ln",
    )(x2, y_mla, y_gm, y_rw, *ws)


def _ffn_kernel(x_ref, wg_ref, wu_ref, wo_ref, g_ref, b_ref, o_ref, xb_sc, acc_sc):
    j = pl.program_id(1)

    @pl.when(j == 0)
    def _():
        xb_sc[...] = x_ref[...].astype(BF16)
        acc_sc[...] = jnp.zeros(acc_sc.shape, F32)

    xb = xb_sc[...]
    gate = jnp.dot(xb, wg_ref[...], preferred_element_type=F32)
    up = jnp.dot(xb, wu_ref[...], preferred_element_type=F32)
    hid = (gate * jax.nn.sigmoid(gate) * up).astype(BF16)
    acc_sc[...] += jnp.dot(hid, wo_ref[...], preferred_element_type=F32)

    @pl.when(j == pl.num_programs(1) - 1)
    def _():
        o_ref[...] = _layer_norm_rows(ALPHA * x_ref[...] + acc_sc[...], g_ref[...], b_ref[...], 1e-5)


def _ffn(x2, lw):
    M, D = x2.shape
    tm = _tile(M, 512)
    tf = 512
    nf = D_FF // tf
    full = lambda a: pl.BlockSpec(a.shape, lambda i, j: (0,) * a.ndim)
    return pl.pallas_call(
        _ffn_kernel,
        grid=(M // tm, nf),
        in_specs=[pl.BlockSpec((tm, D), lambda i, j: (i, 0)),
                  pl.BlockSpec((D, tf), lambda i, j: (0, j)),
                  pl.BlockSpec((D, tf), lambda i, j: (0, j + nf)),
                  pl.BlockSpec((tf, D), lambda i, j: (j, 0)),
                  full(lw['ln2_g']), full(lw['ln2_b'])],
        out_specs=pl.BlockSpec((tm, D), lambda i, j: (i, 0)),
        out_shape=jax.ShapeDtypeStruct((M, D), F32),
        scratch_shapes=[pltpu.VMEM((tm, D), BF16), pltpu.VMEM((tm, D), F32)],
        compiler_params=_cparams(("parallel", "arbitrary")),
        name="ffn",
    )(x2, lw['w_ffn_in'], lw['w_ffn_in'], lw['w_ffn_out'], lw['ln2_g'], lw['ln2_b'])


def _rot_half_cols(w):
    half = QK_ROPE // 2
    return jnp.concatenate([-w[..., half:], w[..., :half]], axis=-1)


def _rope_tables(pos):
    half = QK_ROPE // 2
    inv = ROPE_THETA ** (-jnp.arange(half, dtype=F32) / half)
    ang = pos.astype(F32)[:, None] * inv[None, :]
    cos = jnp.cos(ang)
    sin = jnp.sin(ang)
    cos64 = jnp.concatenate([cos, cos], -1)[None]
    sin64 = jnp.concatenate([sin, sin], -1)[None]
    return cos64, sin64, jnp.tile(cos64, (1, 1, MLA_HEADS)), jnp.tile(sin64, (1, 1, MLA_HEADS))


def _prep_layer(l, P):
    w_in = P['w_in'][l]
    D = w_in.shape[0]
    o_gm = Q_RANK + KV_RANK + QK_ROPE
    o_rw = o_gm + 2 * GM_WIDTH
    w_kr = w_in[:, Q_RANK + KV_RANK:o_gm]
    z64 = jnp.zeros((D, 64), F32)
    w_mla = jnp.concatenate([w_in[:, :Q_RANK + KV_RANK], w_kr, z64, _rot_half_cols(w_kr), z64], axis=1)
    w_uq = P['w_uq'][l]
    w_uq_rope = w_uq[:, :, QK_NOPE:]
    zda = jnp.zeros((D_DECAY, R_WIDTH), F32)
    wda = jnp.concatenate([jnp.concatenate([P['rw_w2'][l], zda], 1),
                           jnp.concatenate([zda, P['rw_a2'][l]], 1)], 0)
    hid = jnp.arange(R_WIDTH) // R_HEAD_DIM
    row = lambda a: a.reshape(1, -1)
    gm_w = P['gm_w_s'][l]
    gm_b = P['gm_b_s'][l]
    return {
        'w_mla': w_mla.astype(BF16),
        'q_norm_g': row(P['q_norm_g'][l]), 'kv_norm_g': row(P['kv_norm_g'][l]),
        'w_uq_nope': w_uq[:, :, :QK_NOPE].reshape(Q_RANK, -1).astype(BF16),
        'w_uq_rope': w_uq_rope.reshape(Q_RANK, -1).astype(BF16),
        'w_uq_rope_rot': _rot_half_cols(w_uq_rope).reshape(Q_RANK, -1).astype(BF16),
        'w_uk_t': jnp.swapaxes(P['w_uk'][l], 1, 2).astype(BF16),
        'w_uv': P['w_uv'][l].astype(BF16),
        'w_gm': w_in[:, o_gm:o_rw].astype(BF16),
        'gm_ln_g': row(P['gm_ln_g'][l]), 'gm_ln_b': row(P['gm_ln_b'][l]),
        'gm_w_s': gm_w,
        'gm_b_full': jnp.repeat(gm_b.T, GM_GROUP_W, axis=1),
        'gm_w00': row(jnp.repeat(gm_w[:, 0, 0], GM_GROUP_W)),
        'gm_b0': row(jnp.repeat(gm_b[:, 0], GM_GROUP_W)),
        'w_rw': w_in[:, o_rw:].astype(BF16),
        'rw_mu': row(P['rw_mu'][l]),
        'rw_wda': wda.astype(BF16),
        'rw_w0a0': row(jnp.concatenate([P['rw_w0'][l], P['rw_a0'][l]])),
        'rw_g2': P['rw_g2'][l].astype(BF16),
        'rw_k_k': row(P['rw_k_k'][l]), 'rw_k_a': row(P['rw_k_a'][l]), 'rw_r_k': row(P['rw_r_k'][l]),
        'rw_ln_g': row(P['rw_ln_g'][l]), 'rw_ln_b': row(P['rw_ln_b'][l]),
        'head_ind': (hid[:, None] == hid[None, :]).astype(BF16),
        'w_out': P['w_out'][l].astype(BF16),
        'ln1_g': row(P['ln1_g'][l]), 'ln1_b': row(P['ln1_b'][l]),
        'w_ffn_in': P['w_ffn_in'][l].astype(BF16),
        'w_ffn_out': P['w_ffn_out'][l].astype(BF16),
        'ln2_g': row(P['ln2_g'][l]), 'ln2_b': row(P['ln2_b'][l]),
    }


def _step_consts():
    n = R_HEAD_DIM
    lane = jnp.arange(n * n)
    et = (jnp.arange(n)[:, None] == (lane % n)[None, :]).astype(BF16)
    er = (jnp.arange(n)[:, None] == (lane // n)[None, :]).astype(BF16)
    return et, er


def _layer_prompt(x, tabs, lw):
    B, T, D = x.shape
    ckv, kr, ckvb, krb, qlat, qrope = _mla_in(x, tabs, lw)
    y_mla = _attn_prompt(qlat, qrope, ckvb, krb, lw['w_uv'])
    x2 = x.reshape(B * T, D)
    y_gm, = _gmlp(x2, lw, single=False)
    shift0 = jnp.zeros((B, 1, R_SHIFT_W), F32)
    r, lgw, k, v, kk, b, g, bonus, shift = _rw_in(x, shift0, lw, seq=True)
    y, S = _wkv_prompt(r, lgw, k, v, kk, b)
    y_rw = _rw_post(y.reshape(B * T, R_WIDTH), bonus.reshape(B * T, R_WIDTH), g.reshape(B * T, R_WIDTH), lw)
    x2 = _out_ln(x2, y_mla.reshape(B * T, MLA_WIDTH), y_gm, y_rw, lw)
    x2 = _ffn(x2, lw)
    return x2.reshape(B, T, D), ckv, kr, S, shift.reshape(B, R_SHIFT_W)


def _layer_sample(l, x, tabs, lw, consts, page_table, cache_c, cache_r, shift0, S0):
    _, Bd, D = x.shape
    ckv, kr, _, _, qlat, qrope = _mla_in(x, tabs, lw)
    tr = lambda a: jnp.swapaxes(a[0], 0, 1)
    y_mla = _attn_sample(l, page_table, tr(qlat), tr(qrope), ckv.reshape(Bd, 1, KV_RANK),
                         kr.reshape(Bd, 1, QK_ROPE), lw['w_uv'], cache_c, cache_r)
    x2 = x.reshape(Bd, D)
    y_gm, v_gm = _gmlp(x2, lw, single=True)
    r, lgw, k, v, kk, b, g, bonus, shift = _rw_in(x, shift0[None], lw, seq=False)
    rows = lambda a: tr(a).reshape(Bd * R_HEADS, R_HEAD_DIM)
    y, S = _wkv_step(rows(r), rows(lgw), rows(k), rows(v), rows(kk), rows(b),
                     S0.reshape(Bd * R_HEADS, R_HEAD_DIM * R_HEAD_DIM), consts)
    y_rw = _rw_post(y.reshape(Bd, R_WIDTH), bonus[0], g[0], lw)
    x2 = _out_ln(x2, y_mla.reshape(Bd, MLA_WIDTH), y_gm, y_rw, lw)
    x2 = _ffn(x2, lw)
    return (x2.reshape(1, Bd, D), ckv.reshape(Bd, 1, KV_RANK), kr.reshape(Bd, 1, QK_ROPE),
            S.reshape(Bd, R_HEADS, R_HEAD_DIM, R_HEAD_DIM), shift[0], v_gm.reshape(Bd, 1, GM_WIDTH))


def kernel(x_prompt, x_sample, cache_c_kv, cache_k_rope, state_rwkv_wkv, state_rwkv_shift, page_table, w_in, q_norm_g, kv_norm_g, w_uq, w_uk, w_uv, gm_ln_g, gm_ln_b, gm_w_s, gm_b_s, rw_mu, rw_w0, rw_w2, rw_a0, rw_a2, rw_g2, rw_k_k, rw_k_a, rw_r_k, rw_ln_g, rw_ln_b, w_out, ln1_g, ln1_b, w_ffn_in, w_ffn_out, ln2_g, ln2_b):
    P = dict(w_in=w_in, q_norm_g=q_norm_g, kv_norm_g=kv_norm_g, w_uq=w_uq, w_uk=w_uk, w_uv=w_uv,
             gm_ln_g=gm_ln_g, gm_ln_b=gm_ln_b, gm_w_s=gm_w_s, gm_b_s=gm_b_s, rw_mu=rw_mu, rw_w0=rw_w0,
             rw_w2=rw_w2, rw_a0=rw_a0, rw_a2=rw_a2, rw_g2=rw_g2, rw_k_k=rw_k_k, rw_k_a=rw_k_a,
             rw_r_k=rw_r_k, rw_ln_g=rw_ln_g, rw_ln_b=rw_ln_b, w_out=w_out, ln1_g=ln1_g, ln1_b=ln1_b,
             w_ffn_in=w_ffn_in, w_ffn_out=w_ffn_out, ln2_g=ln2_g, ln2_b=ln2_b)
    n_dec, t_dec, _ = x_sample.shape
    assert t_dec == 1
    t_prompt = x_prompt.shape[1]
    past_len = page_table.shape[1] * PAGE_SIZE
    tabs_p = _rope_tables(jnp.arange(t_prompt, dtype=jnp.int32))
    tabs_s = _rope_tables(jnp.full((n_dec,), past_len, dtype=jnp.int32))
    consts = _step_consts()
    xp = x_prompt
    xs = x_sample.reshape(1, n_dec, -1)
    outs = [[] for _ in range(9)]
    for l in range(w_in.shape[0]):
        lw = _prep_layer(l, P)
        xp, ckv_p, kr_p, S_p, sh_p = _layer_prompt(xp, tabs_p, lw)
        xs, ckv_s, kr_s, S_s, sh_s, vg_s = _layer_sample(l, xs, tabs_s, lw, consts, page_table, cache_c_kv,
                                                         cache_k_rope, state_rwkv_shift[l], state_rwkv_wkv[l])
        for lst, a in zip(outs, (ckv_p, kr_p, ckv_s, kr_s, S_p, sh_p, S_s, sh_s, vg_s)):
            lst.append(a)
    return (xp, xs.reshape(n_dec, 1, -1)) + tuple(jnp.stack(o) for o in outs)
```

```python
import functools

import jax
import jax.numpy as jnp
from jax import lax
from jax.experimental import pallas as pl
from jax.experimental.pallas import tpu as pltpu

F32 = jnp.float32
BF16 = jnp.bfloat16

D_MODEL = 2048
DEPTH = 4
PAGE_SIZE = 128
MLA_HEADS = 8
QK_NOPE = 128
QK_ROPE = 64
V_HEAD = 128
Q_RANK = 512
KV_RANK = 256
ROPE_THETA = 10000.0
MLA_WIDTH = MLA_HEADS * V_HEAD
SM_SCALE = (QK_NOPE + QK_ROPE) ** -0.5
LOG2_E = 1.4426950408889634
Q_PRESCALE = SM_SCALE * LOG2_E
QK_CAT = KV_RANK + QK_ROPE
GM_GROUPS = 4
GM_CHUNK = 128
GM_GROUP_W = 128
GM_WIDTH = GM_GROUPS * GM_GROUP_W
R_HEADS = 8
R_HEAD_DIM = 64
R_WIDTH = R_HEADS * R_HEAD_DIM
D_DECAY = 64
D_AAA = 64
D_GATE = 128
R_SHIFT_W = 3 * R_WIDTH + D_DECAY + D_AAA + D_GATE
GN_EPS = 64e-5
D_FF = -(-8 * D_MODEL // (3 * 256)) * 256
ALPHA = (2 * DEPTH) ** 0.25
WKV_CHUNK = 64
V7X_VMEM_LIMIT = 56 * 1024 * 1024


def _cparams(sem):
    return pltpu.CompilerParams(dimension_semantics=sem, vmem_limit_bytes=V7X_VMEM_LIMIT)


def _tile(n, pref):
    t = min(n, pref)
    while n % t:
        t //= 2
    return t


def _nt(a, b):
    return lax.dot_general(a, b, (((1,), (1,)), ((), ())), preferred_element_type=F32)


def _split3(x):
    hi = x.astype(BF16)
    r1 = x - hi.astype(F32)
    mid = r1.astype(BF16)
    lo = (r1 - mid.astype(F32)).astype(BF16)
    return hi, mid, lo


def _sel_dot(x, e, nt=False):
    hi, mid, lo = _split3(x)
    d = _nt if nt else functools.partial(jnp.dot, preferred_element_type=F32)
    return (d(hi, e) + d(mid, e)) + d(lo, e)


def _sel_dot_left(e, x):
    hi, mid, lo = _split3(x)
    d = functools.partial(jnp.dot, preferred_element_type=F32)
    return (d(e, hi) + d(e, mid)) + d(e, lo)


def _layer_norm_rows(x, g, b, eps):
    xc = x - jnp.mean(x, -1, keepdims=True)
    var = jnp.mean(xc * xc, -1, keepdims=True)
    return xc * lax.rsqrt(var + eps) * g + b


def _mla_in_kernel(x_ref, w_ref, qg_ref, kvg_ref, cos64_ref, sin64_ref, cos512_ref, sin512_ref,
                   wqn_ref, wqr_ref, wqrr_ref, wuk_ref,
                   ckv_ref, kr_ref, kcat_ref, qcat_ref):
    x = x_ref[0].astype(BF16)
    p = jnp.dot(x, w_ref[...], preferred_element_type=F32)
    cq = p[:, :Q_RANK]
    cq = cq * lax.rsqrt(jnp.mean(cq * cq, -1, keepdims=True) + 1e-6) * qg_ref[...]
    ckv = p[:, Q_RANK:Q_RANK + KV_RANK]
    ckv = ckv * lax.rsqrt(jnp.mean(ckv * ckv, -1, keepdims=True) + 1e-6) * kvg_ref[...]
    kr = p[:, 768:832] * cos64_ref[0] + p[:, 896:960] * sin64_ref[0]
    ckv_ref[0] = ckv
    kr_ref[0] = kr
    kcat_ref[0, :, :KV_RANK] = ckv.astype(BF16)
    kcat_ref[0, :, KV_RANK:] = kr.astype(BF16)
    cqb = cq.astype(BF16)
    qn = jnp.dot(cqb, wqn_ref[...], preferred_element_type=F32)
    qr = jnp.dot(cqb, wqr_ref[...], preferred_element_type=F32)
    qrr = jnp.dot(cqb, wqrr_ref[...], preferred_element_type=F32)
    qrope = (qr * cos512_ref[0] + qrr * sin512_ref[0]) * Q_PRESCALE
    for h in range(MLA_HEADS):
        ql = jnp.dot(qn[:, h * QK_NOPE:(h + 1) * QK_NOPE].astype(BF16), wuk_ref[h],
                     preferred_element_type=F32)
        qcat_ref[0, h, :, :KV_RANK] = (ql * Q_PRESCALE).astype(BF16)
        qcat_ref[0, h, :, KV_RANK:] = qrope[:, h * QK_ROPE:(h + 1) * QK_ROPE].astype(BF16)


def _mla_in(x, tabs, lw):
    B, T, D = x.shape
    tm = _tile(T, 512)
    cos64, sin64, cos512, sin512 = tabs
    row = lambda w: pl.BlockSpec((1, tm, w), lambda b, t: (b, t, 0))
    tab = lambda w: pl.BlockSpec((1, tm, w), lambda b, t: (0, t, 0))
    full = lambda a: pl.BlockSpec(a.shape, lambda b, t: (0,) * a.ndim)
    hd = lambda w: pl.BlockSpec((1, MLA_HEADS, tm, w), lambda b, t: (b, 0, t, 0))
    ws = (lw['w_mla'], lw['q_norm_g'], lw['kv_norm_g'])
    ws2 = (lw['w_uq_nope'], lw['w_uq_rope'], lw['w_uq_rope_rot'], lw['w_uk_t'])
    return pl.pallas_call(
        _mla_in_kernel,
        grid=(B, T // tm),
        in_specs=[row(D)] + [full(a) for a in ws] + [tab(64), tab(64), tab(512), tab(512)] + [full(a) for a in ws2],
        out_specs=[row(KV_RANK), row(QK_ROPE), row(QK_CAT), hd(QK_CAT)],
        out_shape=[jax.ShapeDtypeStruct((B, T, KV_RANK), F32), jax.ShapeDtypeStruct((B, T, QK_ROPE), F32),
                   jax.ShapeDtypeStruct((B, T, QK_CAT), BF16),
                   jax.ShapeDtypeStruct((B, MLA_HEADS, T, QK_CAT), BF16)],
        compiler_params=_cparams(("parallel", "parallel")),
        name="mla_in",
    )(x, *ws, cos64, sin64, cos512, sin512, *ws2)


ATTN_HEAD_GROUP = 2


def _attn_kernel(qi_ref, kj_ref, q_ref, k_ref, wuv_ref, o_ref, m_sc, l_sc, acc_sc, *, tq, tk):
    step = pl.program_id(1)
    qi = qi_ref[step]
    kj = kj_ref[step]
    H = MLA_HEADS
    hg = ATTN_HEAD_GROUP
    n_groups = H // hg
    R = hg * tq

    @pl.when(kj == 0)
    def _():
        m_sc[...] = jnp.full(m_sc.shape, -jnp.inf, F32)
        l_sc[...] = jnp.zeros(l_sc.shape, F32)
        acc_sc[...] = jnp.zeros(acc_sc.shape, F32)

    def update(masked):
        k = k_ref[0]
        kc = k_ref[0, :, :KV_RANK]
        if masked:
            q_pos = qi * tq + lax.broadcasted_iota(jnp.int32, (tq, tk), 0)
            k_pos = kj * tk + lax.broadcasted_iota(jnp.int32, (tq, tk), 1)
            visible = (k_pos <= q_pos)[None]

        def scores(g):
            return _nt(q_ref[0, g * hg:(g + 1) * hg].reshape(R, QK_CAT), k)

        def softmax(g, s):
            rows = slice(g * R, (g + 1) * R)
            if masked:
                s = jnp.where(visible, s.reshape(hg, tq, tk), -jnp.inf).reshape(R, tk)
            m_old = m_sc[rows]
            m_new = jnp.maximum(m_old, jnp.max(s, -1, keepdims=True))
            scale = jnp.exp2(m_old - m_new)
            p = jnp.exp2(s - m_new)
            l_sc[rows] = scale * l_sc[rows] + jnp.sum(p, -1, keepdims=True)
            m_sc[rows] = m_new
            return p.astype(BF16), scale

        def values(g, p, scale):
            rows = slice(g * R, (g + 1) * R)
            acc_sc[rows] = scale * acc_sc[rows] + jnp.dot(p, kc, preferred_element_type=F32)

        s_next = scores(0)
        pending = None
        for g in range(n_groups):
            s_cur = s_next
            if g + 1 < n_groups:
                s_next = scores(g + 1)
            if pending is not None:
                values(*pending)
            pending = (g,) + softmax(g, s_cur)
        values(*pending)

    needs_mask = kj * tk + tk - 1 > qi * tq
    pl.when(needs_mask)(functools.partial(update, True))
    pl.when(jnp.logical_not(needs_mask))(functools.partial(update, False))

    @pl.when(kj == (qi * tq + tq - 1) // tk)
    def _():
        o = (acc_sc[...] / l_sc[...]).astype(BF16)
        for h in range(H):
            o_ref[0, :, h * V_HEAD:(h + 1) * V_HEAD] = jnp.dot(
                o[h * tq:(h + 1) * tq], wuv_ref[h], preferred_element_type=F32).astype(o_ref.dtype)


def _attn_prompt(qcat, kcat, w_uv):
    B, H, T, _ = qcat.shape
    tq = _tile(T, 256)
    tk = _tile(T, 512)
    pairs = [(i, j) for i in range(T // tq) for j in range((i * tq + tq - 1) // tk + 1)]
    qi_of = jnp.asarray([p[0] for p in pairs], jnp.int32)
    kj_of = jnp.asarray([p[1] for p in pairs], jnp.int32)
    grid_spec = pltpu.PrefetchScalarGridSpec(
        num_scalar_prefetch=2,
        grid=(B, len(pairs)),
        in_specs=[pl.BlockSpec((1, H, tq, QK_CAT), lambda b, p, qi, kj: (b, 0, qi[p], 0)),
                  pl.BlockSpec((1, tk, QK_CAT), lambda b, p, qi, kj: (b, kj[p], 0)),
                  pl.BlockSpec(w_uv.shape, lambda b, p, qi, kj: (0, 0, 0))],
        out_specs=pl.BlockSpec((1, tq, MLA_WIDTH), lambda b, p, qi, kj: (b, qi[p], 0)),
        scratch_shapes=[pltpu.VMEM((H * tq, 1), F32), pltpu.VMEM((H * tq, 1), F32),
                        pltpu.VMEM((H * tq, KV_RANK), F32)])
    return pl.pallas_call(
        functools.partial(_attn_kernel, tq=tq, tk=tk),
        grid_spec=grid_spec,
        out_shape=jax.ShapeDtypeStruct((B, T, MLA_WIDTH), BF16),
        compiler_params=_cparams(("parallel", "arbitrary")),
        name="attn_prompt",
    )(qi_of, kj_of, qcat, kcat, w_uv)


def _attn_sample_kernel(pt_ref, q_ref, cnew_ref, rnew_ref, wuv_ref, cc_hbm, cr_hbm, o_ref,
                        cbuf, rbuf, sem, s_sc, kb_sc, *, layer, n_pages, chunk):
    b = pl.program_id(0)
    nb = pl.num_programs(0)
    H = MLA_HEADS
    n_keys = n_pages * PAGE_SIZE

    def copies(bb, slot):
        out = []
        for j in range(n_pages):
            pg = pt_ref[bb, j]
            keys = pl.ds(j * PAGE_SIZE, PAGE_SIZE)
            out.append(pltpu.make_async_copy(cc_hbm.at[layer, pg], cbuf.at[slot, keys], sem.at[0, slot]))
            out.append(pltpu.make_async_copy(cr_hbm.at[layer, pg], rbuf.at[slot, :, keys], sem.at[1, slot]))
        return out

    @pl.when(b == 0)
    def _():
        for c in copies(0, 0):
            c.start()

    slot = lax.rem(b, 2)

    @pl.when(b + 1 < nb)
    def _():
        for c in copies(b + 1, 1 - slot):
            c.start()

    for c in copies(b, slot):
        c.wait()

    ql = q_ref[0, :, :KV_RANK]
    qr = q_ref[0, :, KV_RANK:]
    for c in range(n_keys // chunk):
        keys = slice(c * chunk, (c + 1) * chunk)
        kc = cbuf[slot, keys].astype(BF16)
        kb_sc[keys] = kc
        s_rope = jnp.dot(qr, rbuf[slot, :, keys].astype(BF16), preferred_element_type=F32)
        s_sc[:, keys] = _nt(ql, kc) + s_rope
    cn = cnew_ref[0].astype(BF16)
    s_new = (jnp.sum(ql.astype(F32) * cn.astype(F32), -1, keepdims=True)
             + jnp.sum(qr.astype(F32) * rnew_ref[0].astype(BF16).astype(F32), -1, keepdims=True))
    s = s_sc[...]
    m = jnp.maximum(jnp.max(s, -1, keepdims=True), s_new)
    p = jnp.exp2(s - m)
    p_new = jnp.exp2(s_new - m)
    l = jnp.sum(p, -1, keepdims=True) + p_new
    p = (p / l).astype(BF16)
    acc = (p_new / l).astype(BF16).astype(F32) * cn.astype(F32)
    for c in range(n_keys // chunk):
        keys = slice(c * chunk, (c + 1) * chunk)
        acc = acc + jnp.dot(p[:, keys], kb_sc[keys], preferred_element_type=F32)
    ob = acc.astype(BF16)
    for h in range(H):
        o_ref[0, :, h * V_HEAD:(h + 1) * V_HEAD] = jnp.dot(
            ob[h:h + 1], wuv_ref[h], preferred_element_type=F32).astype(o_ref.dtype)


def _attn_sample(layer, page_table, qcat, ckv_new, kr_new, w_uv, cache_c, cache_rt):
    B, n_pages = page_table.shape
    H = MLA_HEADS
    n_keys = n_pages * PAGE_SIZE
    chunk = _tile(n_keys, 1024)
    grid_spec = pltpu.PrefetchScalarGridSpec(
        num_scalar_prefetch=1,
        grid=(B,),
        in_specs=[pl.BlockSpec((1, H, QK_CAT), lambda b, pt: (b, 0, 0)),
                  pl.BlockSpec((1, 1, KV_RANK), lambda b, pt: (b, 0, 0)),
                  pl.BlockSpec((1, 1, QK_ROPE), lambda b, pt: (b, 0, 0)),
                  pl.BlockSpec(w_uv.shape, lambda b, pt: (0, 0, 0)),
                  pl.BlockSpec(memory_space=pl.ANY),
                  pl.BlockSpec(memory_space=pl.ANY)],
        out_specs=pl.BlockSpec((1, 1, MLA_WIDTH), lambda b, pt: (b, 0, 0)),
        scratch_shapes=[pltpu.VMEM((2, n_keys, KV_RANK), F32),
                        pltpu.VMEM((2, QK_ROPE, n_keys), F32),
                        pltpu.SemaphoreType.DMA((2, 2)),
                        pltpu.VMEM((H, n_keys), F32),
                        pltpu.VMEM((n_keys, KV_RANK), BF16)])
    return pl.pallas_call(
        functools.partial(_attn_sample_kernel, layer=layer, n_pages=n_pages, chunk=chunk),
        grid_spec=grid_spec,
        out_shape=jax.ShapeDtypeStruct((B, 1, MLA_WIDTH), BF16),
        compiler_params=_cparams(("arbitrary",)),
        name="attn_sample",
    )(page_table, qcat, ckv_new, kr_new, w_uv, cache_c, cache_rt)


def _gelu_tanh(x):
    return 0.5 * x * (1.0 + jnp.tanh(0.7978845608028654 * (x + 0.044715 * (x * x * x))))


def _gm_kernel(x_ref, w_ref, lng_ref, lnb_ref, ws_ref, bs_ref, y_ref, *v_out, tm, single):
    x = x_ref[...].astype(BF16)
    z = _gelu_tanh(jnp.dot(x, w_ref[...], preferred_element_type=F32))
    u = z[:, :GM_WIDTH]
    v = _layer_norm_rows(z[:, GM_WIDTH:], lng_ref[...], lnb_ref[...], 1e-5)
    if single:
        y_ref[...] = (u * (v * ws_ref[...] + bs_ref[...])).astype(y_ref.dtype)
        v_out[0][...] = v
        return
    t_i = lax.broadcasted_iota(jnp.int32, (GM_CHUNK, GM_CHUNK), 0)
    s_i = lax.broadcasted_iota(jnp.int32, (GM_CHUNK, GM_CHUNK), 1)
    wm = [jnp.where(s_i <= t_i, ws_ref[g], 0.0).astype(BF16) for g in range(GM_GROUPS)]
    vb = v.astype(BF16)
    for c in range(tm // GM_CHUNK):
        rows = slice(c * GM_CHUNK, (c + 1) * GM_CHUNK)
        for g in range(GM_GROUPS):
            cols = slice(g * GM_GROUP_W, (g + 1) * GM_GROUP_W)
            mixed = jnp.dot(wm[g], vb[rows, cols], preferred_element_type=F32) + bs_ref[:, cols]
            y_ref[rows, cols] = (u[rows, cols] * mixed).astype(y_ref.dtype)


def _gmlp(x2, lw, single):
    M, D = x2.shape
    tm = _tile(M, 256)
    ws, bs = (lw['gm_w00'], lw['gm_b0']) if single else (lw['gm_w_s'], lw['gm_b_full'])
    full = lambda a: pl.BlockSpec(a.shape, lambda i: (0,) * a.ndim)
    row = lambda w: pl.BlockSpec((tm, w), lambda i: (i, 0))
    out_specs = [row(GM_WIDTH)]
    out_shape = [jax.ShapeDtypeStruct((M, GM_WIDTH), BF16)]
    if single:
        out_specs.append(row(GM_WIDTH))
        out_shape.append(jax.ShapeDtypeStruct((M, GM_WIDTH), F32))
    ins = (lw['w_gm'], lw['gm_ln_g'], lw['gm_ln_b'], ws, bs)
    return pl.pallas_call(
        functools.partial(_gm_kernel, tm=tm, single=single),
        grid=(M // tm,),
        in_specs=[row(D)] + [full(a) for a in ins],
        out_specs=out_specs, out_shape=out_shape,
        compiler_params=_cparams(("parallel",)),
        name="gmlp",
    )(x2, *ins)


def _rw_in_kernel(x_ref, prev_ref, w_ref, mu_ref, wda_ref, w0a0_ref, g2_ref, kk_ref_w, ka_ref, rk_ref, e_ref,
                  *refs, tm, seq):
    outs, (g_o, bonus_o, shift_o, carry) = refs[:6], refs[6:]
    t = pl.program_id(1)
    x = x_ref[0].astype(BF16)
    p = jnp.dot(x, w_ref[...], preferred_element_type=F32)
    if seq:
        @pl.when(t == 0)
        def _():
            carry[...] = prev_ref[0]
        row = lax.broadcasted_iota(jnp.int32, (tm, 1), 0)
        prev = jnp.where(row == 0, carry[...], pltpu.roll(p, 1, axis=0))
        carry[...] = p[tm - 1:tm]
    else:
        prev = prev_ref[0]
    shift_o[0] = p[tm - 1:tm] if seq else p
    xm = p + (prev - p) * mu_ref[...]
    r = xm[:, :R_WIDTH]
    k = xm[:, R_WIDTH:2 * R_WIDTH]
    v = xm[:, 2 * R_WIDTH:3 * R_WIDTH]
    da = xm[:, 3 * R_WIDTH:3 * R_WIDTH + D_DECAY + D_AAA]
    lane = lax.broadcasted_iota(jnp.int32, da.shape, 1)
    da = jnp.where(lane < D_DECAY, jnp.tanh(da), da)
    m = jnp.dot(da.astype(BF16), wda_ref[...], preferred_element_type=F32) + w0a0_ref[...]
    wv = m[:, :R_WIDTH]
    w_log = -(jnp.maximum(-wv, 0.0) + jnp.log1p(jnp.exp(-jnp.abs(wv)))) - 0.5
    logdecay = -jnp.exp(w_log)
    a = jax.nn.sigmoid(m[:, R_WIDTH:])
    gd = xm[:, 3 * R_WIDTH + D_DECAY + D_AAA:]
    g = jnp.dot(jax.nn.sigmoid(gd).astype(BF16), g2_ref[...], preferred_element_type=F32)
    e = e_ref[...]
    kk = k * kk_ref_w[...]
    kk = kk * lax.rsqrt(jnp.maximum(_sel_dot(kk * kk, e), 1e-24))
    k2 = k * (1.0 + (a - 1.0) * ka_ref[...])
    bonus = _sel_dot(r * k2 * rk_ref[...], e) * v
    g_o[0] = g
    bonus_o[0] = bonus
    bb = kk * a
    if seq:
        C = WKV_CHUNK
        ti = lax.broadcasted_iota(jnp.int32, (tm, tm), 0)
        si = lax.broadcasted_iota(jnp.int32, (tm, tm), 1)
        tri = jnp.where((ti // C == si // C) & (si <= ti), 1.0, 0.0).astype(BF16)
        ci = lax.broadcasted_iota(jnp.int32, (tm // C, tm), 0)
        sj = lax.broadcasted_iota(jnp.int32, (tm // C, tm), 1)
        in_chunk = jnp.where(sj // C == ci, 1.0, 0.0).astype(BF16)
        cum = _sel_dot_left(tri, logdecay)
        p_in = jnp.exp(cum)
        p_inv = jnp.exp(-cum)
        pc_o = outs[5]
        pc_o[0] = jnp.exp(_sel_dot_left(in_chunk, logdecay))
        vals = (-kk * jnp.exp(cum - logdecay), bb * p_inv, k2 * p_inv, r * p_in, v)
    else:
        vals = (r, logdecay, k2, v, kk, bb)
    for h in range(R_HEADS):
        c = slice(h * R_HEAD_DIM, (h + 1) * R_HEAD_DIM)
        for o_ref, val in zip(outs, vals):
            o_ref[0, h] = val[:, c].astype(o_ref.dtype)


def _rw_in(x, prev, lw, seq):
    B, T, D = x.shape
    tm = _tile(T, 512 if seq else 256)
    row = lambda w: pl.BlockSpec((1, tm, w), lambda b, t: (b, t, 0))
    full = lambda a: pl.BlockSpec(a.shape, lambda b, t: (0,) * a.ndim)
    hd = pl.BlockSpec((1, R_HEADS, tm, R_HEAD_DIM), lambda b, t: (b, 0, t, 0))
    ws = (lw['w_rw'], lw['rw_mu'], lw['rw_wda'], lw['rw_w0a0'], lw['rw_g2'], lw['rw_k_k'], lw['rw_k_a'],
          lw['rw_r_k'], lw['head_ind'])
    hshape = lambda dt: jax.ShapeDtypeStruct((B, R_HEADS, T, R_HEAD_DIM), dt)
    if seq:
        C = WKV_CHUNK
        prev_spec = pl.BlockSpec((1, 1, R_SHIFT_W), lambda b, t: (b, 0, 0))
        shift_spec = pl.BlockSpec((1, 1, R_SHIFT_W), lambda b, t: (b, 0, 0))
        shift_shape = jax.ShapeDtypeStruct((B, 1, R_SHIFT_W), F32)
        head_specs = [hd] * 5 + [pl.BlockSpec((1, tm // C, R_WIDTH), lambda b, t: (b, t, 0))]
        head_shapes = [hshape(BF16)] * 5 + [jax.ShapeDtypeStruct((B, T // C, R_WIDTH), F32)]
    else:
        prev_spec = row(R_SHIFT_W)
        shift_spec = row(R_SHIFT_W)
        shift_shape = jax.ShapeDtypeStruct((B, T, R_SHIFT_W), F32)
        head_specs = [hd] * 6
        head_shapes = [hshape(F32)] * 6
    fshape = jax.ShapeDtypeStruct((B, T, R_WIDTH), F32)
    return pl.pallas_call(
        functools.partial(_rw_in_kernel, tm=tm, seq=seq),
        grid=(B, T // tm),
        in_specs=[row(D), prev_spec] + [full(a) for a in ws],
        out_specs=head_specs + [row(R_WIDTH), row(R_WIDTH), shift_spec],
        out_shape=head_shapes + [fshape, fshape, shift_shape],
        scratch_shapes=[pltpu.VMEM((1, R_SHIFT_W), F32)],
        compiler_params=_cparams(("parallel", "arbitrary")),
        name="rw_in",
    )(x, prev, *ws)


def _bdot(a, b):
    return jnp.dot(a.astype(BF16), b.astype(BF16), preferred_element_type=F32)


def _bdot_nt(a, b):
    return _nt(a.astype(BF16), b.astype(BF16))


def _bdot_tn(a, b):
    return lax.dot_general(a.astype(BF16), b.astype(BF16), (((0,), (0,)), ((), ())), preferred_element_type=F32)


def _wkv_kernel(a_ref, b_ref, k_ref, r_ref, v_ref, pc_ref, y_ref, s_ref, state):
    c = pl.program_id(0)
    n_seq, n_head, C, _ = a_ref.shape

    @pl.when(c == 0)
    def _():
        state[...] = jnp.zeros(state.shape, F32)

    ti = lax.broadcasted_iota(jnp.int32, (C, C), 0)
    si = lax.broadcasted_iota(jnp.int32, (C, C), 1)
    incl = si <= ti
    strict = si < ti
    n_steps = C.bit_length() - 1
    ch = [(b, h) for b in range(n_seq) for h in range(n_head)]
    each = lambda f, *lists: [f(*args) for args in zip(*lists)]
    A = [a_ref[b, h] for b, h in ch]
    Bm = [b_ref[b, h] for b, h in ch]
    K = [k_ref[b, h] for b, h in ch]
    R = [r_ref[b, h] for b, h in ch]
    V = [v_ref[b, h] for b, h in ch]
    Nb = each(lambda a, bm: jnp.where(strict, _nt(a, bm), 0.0).astype(BF16), A, Bm)
    Mm = each(lambda a, k: jnp.where(strict, _nt(a, k), 0.0), A, K)
    XA = [a.astype(F32) for a in A]
    XU = each(_bdot, Mm, V)
    for i in range(n_steps):
        XA = each(lambda n, x: x + _bdot(n, x), Nb, XA)
        XU = each(lambda n, x: x + _bdot(n, x), Nb, XU)
        if i + 1 < n_steps:
            Nb = each(lambda n: _bdot(n, n).astype(BF16), Nb)
    RB = each(lambda r, bm: jnp.where(incl, _nt(r, bm), 0.0).astype(BF16), R, Bm)
    RK = each(lambda r, k: jnp.where(incl, _nt(r, k), 0.0), R, K)
    Q = each(lambda r, rb, xa: r.astype(F32) + _bdot(rb, xa), R, RB, XA)
    Y0 = each(lambda rb, xu, rk, v: _bdot(rb, xu) + _bdot(rk, v), RB, XU, RK, V)
    G = each(_bdot_tn, XA, Bm)
    Hm = each(lambda xu, bm, v, k: _bdot_tn(xu, bm) + _bdot_tn(v, k), XU, Bm, V, K)
    S0 = [state[b, h] for b, h in ch]
    Y = each(lambda q, s0, y0: _bdot_nt(q, s0) + y0, Q, S0, Y0)
    pc = [pc_ref[b, 0, :, h * R_HEAD_DIM:(h + 1) * R_HEAD_DIM] for b, h in ch]
    S1 = each(lambda s0, g, hm, p: (s0 + _bdot(s0, g) + hm) * p, S0, G, Hm, pc)
    for (b, h), s1 in zip(ch, S1):
        state[b, h] = s1
        s_ref[b, h] = s1
    for b in range(n_seq):
        y_ref[b] = jnp.concatenate(Y[b * n_head:(b + 1) * n_head], axis=-1)


def _wkv_prompt(A, Bm, K, R, V, pc):
    B, H, T, N = A.shape
    C = WKV_CHUNK
    hd = pl.BlockSpec((B, H, C, N), lambda c: (0, 0, c, 0))
    return pl.pallas_call(
        _wkv_kernel,
        grid=(T // C,),
        in_specs=[hd] * 5 + [pl.BlockSpec((B, 1, 1, H * N), lambda c: (0, c, 0, 0))],
        out_specs=[pl.BlockSpec((B, C, H * N), lambda c: (0, c, 0)),
                   pl.BlockSpec((B, H, N, N), lambda c: (0, 0, 0, 0))],
        out_shape=[jax.ShapeDtypeStruct((B, T, H * N), F32), jax.ShapeDtypeStruct((B, H, N, N), F32)],
        scratch_shapes=[pltpu.VMEM((B, H, N, N), F32)],
        compiler_params=_cparams(("arbitrary",)),
        name="wkv_prompt",
    )(A, Bm, K, R, V, pc.reshape(B, T // C, 1, H * N))


def _wkv_step_kernel(r_ref, lw_ref, k_ref, v_ref, kk_ref, b_ref, s_ref, et_ref, er_ref, y_ref, so_ref):
    et = et_ref[...]
    er = er_ref[...]
    S = s_ref[...]
    over_j = lambda x: _sel_dot(x, et)
    over_i = lambda x: _sel_dot(x, er)
    red_j = lambda x: _sel_dot(x, er, nt=True)
    sa = red_j(S * over_j(-kk_ref[...]))
    S1 = (S * over_j(jnp.exp(lw_ref[...])) + over_i(sa) * over_j(b_ref[...])
          + over_i(v_ref[...]) * over_j(k_ref[...]))
    so_ref[...] = S1
    y_ref[...] = red_j(S1 * over_j(r_ref[...]))


def _wkv_step(r, lw, k, v, kk, b, S, lw_consts):
    R, N = r.shape
    tr = _tile(R, 128)
    et, er = lw_consts
    row = lambda w: pl.BlockSpec((tr, w), lambda i: (i, 0))
    full = lambda a: pl.BlockSpec(a.shape, lambda i: (0,) * a.ndim)
    return pl.pallas_call(
        _wkv_step_kernel,
        grid=(R // tr,),
        in_specs=[row(N)] * 6 + [row(N * N), full(et), full(er)],
        out_specs=[row(N), row(N * N)],
        out_shape=[jax.ShapeDtypeStruct((R, N), F32), jax.ShapeDtypeStruct((R, N * N), F32)],
        compiler_params=_cparams(("parallel",)),
        name="wkv_step",
    )(r, lw, k, v, kk, b, S, et, er)


def _rw_post_kernel(y_ref, bonus_ref, g_ref, lng_ref, lnb_ref, e_ref, o_ref):
    e = e_ref[...]
    y = y_ref[...]
    mean = _sel_dot(y, e) * (1.0 / R_HEAD_DIM)
    yc = y - mean
    var = _sel_dot(yc * yc, e) * (1.0 / R_HEAD_DIM)
    yn = yc * lax.rsqrt(var + GN_EPS) * lng_ref[...] + lnb_ref[...]
    o_ref[...] = ((yn + bonus_ref[...]) * g_ref[...]).astype(o_ref.dtype)


def _rw_post(y, bonus, g, lw):
    M, W = y.shape
    tm = _tile(M, 512)
    row = pl.BlockSpec((tm, W), lambda i: (i, 0))
    full = lambda a: pl.BlockSpec(a.shape, lambda i: (0,) * a.ndim)
    ws = (lw['rw_ln_g'], lw['rw_ln_b'], lw['head_ind'])
    return pl.pallas_call(
        _rw_post_kernel,
        grid=(M // tm,),
        in_specs=[row, row, row] + [full(a) for a in ws],
        out_specs=row,
        out_shape=jax.ShapeDtypeStruct((M, W), BF16),
        compiler_params=_cparams(("parallel",)),
        name="rw_post",
    )(y, bonus, g, *ws)


def _out_ln_kernel(x_ref, ya_ref, yg_ref, yr_ref, w_ref, g_ref, b_ref, o_ref):
    h = jnp.dot(ya_ref[...], w_ref[:MLA_WIDTH], preferred_element_type=F32)
    h = h + jnp.dot(yg_ref[...], w_ref[MLA_WIDTH:MLA_WIDTH + GM_WIDTH], preferred_element_type=F32)
    h = h + jnp.dot(yr_ref[...], w_ref[MLA_WIDTH + GM_WIDTH:], preferred_element_type=F32)
    o_ref[...] = _layer_norm_rows(ALPHA * x_ref[...] + h, g_ref[...], b_ref[...], 1e-5)


def _out_ln(x2, y_mla, y_gm, y_rw, lw):
    M, D = x2.shape
    tm = _tile(M, 512)
    row = lambda w: pl.BlockSpec((tm, w), lambda i: (i, 0))
    full = lambda a: pl.BlockSpec(a.shape, lambda i: (0,) * a.ndim)
    ws = (lw['w_out'], lw['ln1_g'], lw['ln1_b'])
    return pl.pallas_call(
        _out_ln_kernel,
        grid=(M // tm,),
        in_specs=[row(D), row(MLA_WIDTH), row(GM_WIDTH), row(R_WIDTH)] + [full(a) for a in ws],
        out_specs=row(D),
        out_shape=jax.ShapeDtypeStruct((M, D), F32),
        compiler_params=_cparams(("parallel",)),
        name="out_ln",
    )(x2, y_mla, y_gm, y_rw, *ws)


def _ffn_kernel(x_ref, wg_ref, wu_ref, wo_ref, g_ref, b_ref, o_ref, xb_sc, acc_sc):
    j = pl.program_id(1)

    @pl.when(j == 0)
    def _():
        xb_sc[...] = x_ref[...].astype(BF16)
        acc_sc[...] = jnp.zeros(acc_sc.shape, F32)

    xb = xb_sc[...]
    gate = jnp.dot(xb, wg_ref[...], preferred_element_type=F32)
    up = jnp.dot(xb, wu_ref[...], preferred_element_type=F32)
    hid = (gate * jax.nn.sigmoid(gate) * up).astype(BF16)
    acc_sc[...] += jnp.dot(hid, wo_ref[...], preferred_element_type=F32)

    @pl.when(j == pl.num_programs(1) - 1)
    def _():
        o_ref[...] = _layer_norm_rows(ALPHA * x_ref[...] + acc_sc[...], g_ref[...], b_ref[...], 1e-5)


def _ffn(x2, lw):
    M, D = x2.shape
    tm = _tile(M, 512)
    tf = 512
    nf = D_FF // tf
    full = lambda a: pl.BlockSpec(a.shape, lambda i, j: (0,) * a.ndim)
    return pl.pallas_call(
        _ffn_kernel,
        grid=(M // tm, nf),
        in_specs=[pl.BlockSpec((tm, D), lambda i, j: (i, 0)),
                  pl.BlockSpec((D, tf), lambda i, j: (0, j)),
                  pl.BlockSpec((D, tf), lambda i, j: (0, j + nf)),
                  pl.BlockSpec((tf, D), lambda i, j: (j, 0)),
                  full(lw['ln2_g']), full(lw['ln2_b'])],
        out_specs=pl.BlockSpec((tm, D), lambda i, j: (i, 0)),
        out_shape=jax.ShapeDtypeStruct((M, D), F32),
        scratch_shapes=[pltpu.VMEM((tm, D), BF16), pltpu.VMEM((tm, D), F32)],
        compiler_params=_cparams(("parallel", "arbitrary")),
        name="ffn",
    )(x2, lw['w_ffn_in'], lw['w_ffn_in'], lw['w_ffn_out'], lw['ln2_g'], lw['ln2_b'])


def _rot_half_cols(w):
    half = QK_ROPE // 2
    return jnp.concatenate([-w[..., half:], w[..., :half]], axis=-1)


def _rope_tables(pos):
    half = QK_ROPE // 2
    inv = ROPE_THETA ** (-jnp.arange(half, dtype=F32) / half)
    ang = pos.astype(F32)[:, None] * inv[None, :]
    cos = jnp.cos(ang)
    sin = jnp.sin(ang)
    cos64 = jnp.concatenate([cos, cos], -1)[None]
    sin64 = jnp.concatenate([sin, sin], -1)[None]
    return cos64, sin64, jnp.tile(cos64, (1, 1, MLA_HEADS)), jnp.tile(sin64, (1, 1, MLA_HEADS))


def _prep_layer(l, P):
    w_in = P['w_in'][l]
    D = w_in.shape[0]
    o_gm = Q_RANK + KV_RANK + QK_ROPE
    o_rw = o_gm + 2 * GM_WIDTH
    w_kr = w_in[:, Q_RANK + KV_RANK:o_gm]
    z64 = jnp.zeros((D, 64), F32)
    w_mla = jnp.concatenate([w_in[:, :Q_RANK + KV_RANK], w_kr, z64, _rot_half_cols(w_kr), z64], axis=1)
    w_uq = P['w_uq'][l]
    w_uq_rope = w_uq[:, :, QK_NOPE:]
    zda = jnp.zeros((D_DECAY, R_WIDTH), F32)
    wda = jnp.concatenate([jnp.concatenate([P['rw_w2'][l], zda], 1),
                           jnp.concatenate([zda, P['rw_a2'][l]], 1)], 0)
    hid = jnp.arange(R_WIDTH) // R_HEAD_DIM
    row = lambda a: a.reshape(1, -1)
    gm_w = P['gm_w_s'][l]
    gm_b = P['gm_b_s'][l]
    return {
        'w_mla': w_mla.astype(BF16),
        'q_norm_g': row(P['q_norm_g'][l]), 'kv_norm_g': row(P['kv_norm_g'][l]),
        'w_uq_nope': w_uq[:, :, :QK_NOPE].reshape(Q_RANK, -1).astype(BF16),
        'w_uq_rope': w_uq_rope.reshape(Q_RANK, -1).astype(BF16),
        'w_uq_rope_rot': _rot_half_cols(w_uq_rope).reshape(Q_RANK, -1).astype(BF16),
        'w_uk_t': jnp.swapaxes(P['w_uk'][l], 1, 2).astype(BF16),
        'w_uv': P['w_uv'][l].astype(BF16),
        'w_gm': w_in[:, o_gm:o_rw].astype(BF16),
        'gm_ln_g': row(P['gm_ln_g'][l]), 'gm_ln_b': row(P['gm_ln_b'][l]),
        'gm_w_s': gm_w,
        'gm_b_full': jnp.repeat(gm_b.T, GM_GROUP_W, axis=1),
        'gm_w00': row(jnp.repeat(gm_w[:, 0, 0], GM_GROUP_W)),
        'gm_b0': row(jnp.repeat(gm_b[:, 0], GM_GROUP_W)),
        'w_rw': w_in[:, o_rw:].astype(BF16),
        'rw_mu': row(P['rw_mu'][l]),
        'rw_wda': wda.astype(BF16),
        'rw_w0a0': row(jnp.concatenate([P['rw_w0'][l], P['rw_a0'][l]])),
        'rw_g2': P['rw_g2'][l].astype(BF16),
        'rw_k_k': row(P['rw_k_k'][l]), 'rw_k_a': row(P['rw_k_a'][l]), 'rw_r_k': row(P['rw_r_k'][l]),
        'rw_ln_g': row(P['rw_ln_g'][l]), 'rw_ln_b': row(P['rw_ln_b'][l]),
        'head_ind': (hid[:, None] == hid[None, :]).astype(BF16),
        'w_out': P['w_out'][l].astype(BF16),
        'ln1_g': row(P['ln1_g'][l]), 'ln1_b': row(P['ln1_b'][l]),
        'w_ffn_in': P['w_ffn_in'][l].astype(BF16),
        'w_ffn_out': P['w_ffn_out'][l].astype(BF16),
        'ln2_g': row(P['ln2_g'][l]), 'ln2_b': row(P['ln2_b'][l]),
    }


def _step_consts():
    n = R_HEAD_DIM
    lane = jnp.arange(n * n)
    et = (jnp.arange(n)[:, None] == (lane % n)[None, :]).astype(BF16)
    er = (jnp.arange(n)[:, None] == (lane // n)[None, :]).astype(BF16)
    return et, er


def _layer_prompt(x, tabs, lw):
    B, T, D = x.shape
    ckv, kr, kcat, qcat = _mla_in(x, tabs, lw)
    y_mla = _attn_prompt(qcat, kcat, lw['w_uv'])
    x2 = x.reshape(B * T, D)
    y_gm, = _gmlp(x2, lw, single=False)
    shift0 = jnp.zeros((B, 1, R_SHIFT_W), F32)
    A, Bm, K, R, V, pc, g, bonus, shift = _rw_in(x, shift0, lw, seq=True)
    y, S = _wkv_prompt(A, Bm, K, R, V, pc)
    y_rw = _rw_post(y.reshape(B * T, R_WIDTH), bonus.reshape(B * T, R_WIDTH), g.reshape(B * T, R_WIDTH), lw)
    x2 = _out_ln(x2, y_mla.reshape(B * T, MLA_WIDTH), y_gm, y_rw, lw)
    x2 = _ffn(x2, lw)
    return x2.reshape(B, T, D), ckv, kr, S, shift.reshape(B, R_SHIFT_W)


def _layer_sample(l, x, tabs, lw, consts, page_table, cache_c, cache_r, shift0, S0):
    _, Bd, D = x.shape
    ckv, kr, _, qcat = _mla_in(x, tabs, lw)
    tr = lambda a: jnp.swapaxes(a[0], 0, 1)
    y_mla = _attn_sample(l, page_table, tr(qcat), ckv.reshape(Bd, 1, KV_RANK),
                         kr.reshape(Bd, 1, QK_ROPE), lw['w_uv'], cache_c, cache_r)
    x2 = x.reshape(Bd, D)
    y_gm, v_gm = _gmlp(x2, lw, single=True)
    r, lgw, k, v, kk, b, g, bonus, shift = _rw_in(x, shift0[None], lw, seq=False)
    rows = lambda a: tr(a).reshape(Bd * R_HEADS, R_HEAD_DIM)
    y, S = _wkv_step(rows(r), rows(lgw), rows(k), rows(v), rows(kk), rows(b),
                     S0.reshape(Bd * R_HEADS, R_HEAD_DIM * R_HEAD_DIM), consts)
    y_rw = _rw_post(y.reshape(Bd, R_WIDTH), bonus[0], g[0], lw)
    x2 = _out_ln(x2, y_mla.reshape(Bd, MLA_WIDTH), y_gm, y_rw, lw)
    x2 = _ffn(x2, lw)
    return (x2.reshape(1, Bd, D), ckv.reshape(Bd, 1, KV_RANK), kr.reshape(Bd, 1, QK_ROPE),
            S.reshape(Bd, R_HEADS, R_HEAD_DIM, R_HEAD_DIM), shift[0], v_gm.reshape(Bd, 1, GM_WIDTH))


def kernel(x_prompt, x_sample, cache_c_kv, cache_k_rope, state_rwkv_wkv, state_rwkv_shift, page_table, w_in, q_norm_g, kv_norm_g, w_uq, w_uk, w_uv, gm_ln_g, gm_ln_b, gm_w_s, gm_b_s, rw_mu, rw_w0, rw_w2, rw_a0, rw_a2, rw_g2, rw_k_k, rw_k_a, rw_r_k, rw_ln_g, rw_ln_b, w_out, ln1_g, ln1_b, w_ffn_in, w_ffn_out, ln2_g, ln2_b):
    P = dict(w_in=w_in, q_norm_g=q_norm_g, kv_norm_g=kv_norm_g, w_uq=w_uq, w_uk=w_uk, w_uv=w_uv,
             gm_ln_g=gm_ln_g, gm_ln_b=gm_ln_b, gm_w_s=gm_w_s, gm_b_s=gm_b_s, rw_mu=rw_mu, rw_w0=rw_w0,
             rw_w2=rw_w2, rw_a0=rw_a0, rw_a2=rw_a2, rw_g2=rw_g2, rw_k_k=rw_k_k, rw_k_a=rw_k_a,
             rw_r_k=rw_r_k, rw_ln_g=rw_ln_g, rw_ln_b=rw_ln_b, w_out=w_out, ln1_g=ln1_g, ln1_b=ln1_b,
             w_ffn_in=w_ffn_in, w_ffn_out=w_ffn_out, ln2_g=ln2_g, ln2_b=ln2_b)
    n_dec, t_dec, _ = x_sample.shape
    assert t_dec == 1
    t_prompt = x_prompt.shape[1]
    past_len = page_table.shape[1] * PAGE_SIZE
    tabs_p = _rope_tables(jnp.arange(t_prompt, dtype=jnp.int32))
    tabs_s = _rope_tables(jnp.full((n_dec,), past_len, dtype=jnp.int32))
    consts = _step_consts()
    cache_k_rope_t = jnp.swapaxes(cache_k_rope, 2, 3)
    xp = x_prompt
    xs = x_sample.reshape(1, n_dec, -1)
    outs = [[] for _ in range(9)]
    for l in range(w_in.shape[0]):
        lw = _prep_layer(l, P)
        xp, ckv_p, kr_p, S_p, sh_p = _layer_prompt(xp, tabs_p, lw)
        xs, ckv_s, kr_s, S_s, sh_s, vg_s = _layer_sample(l, xs, tabs_s, lw, consts, page_table, cache_c_kv,
                                                         cache_k_rope_t, state_rwkv_shift[l], state_rwkv_wkv[l])
        for lst, a in zip(outs, (ckv_p, kr_p, ckv_s, kr_s, S_p, sh_p, S_s, sh_s, vg_s)):
            lst.append(a)
    return (xp, xs.reshape(n_dec, 1, -1)) + tuple(jnp.stack(o) for o in outs)
```

```python
import functools

import jax
import jax.numpy as jnp
from jax import lax
from jax.experimental import pallas as pl
from jax.experimental.pallas import tpu as pltpu

F32 = jnp.float32
BF16 = jnp.bfloat16

D_MODEL = 2048
DEPTH = 4
PAGE_SIZE = 128
MLA_HEADS = 8
QK_NOPE = 128
QK_ROPE = 64
V_HEAD = 128
Q_RANK = 512
KV_RANK = 256
ROPE_THETA = 10000.0
MLA_WIDTH = MLA_HEADS * V_HEAD
SM_SCALE = (QK_NOPE + QK_ROPE) ** -0.5
LOG2_E = 1.4426950408889634
Q_PRESCALE = SM_SCALE * LOG2_E
QK_CAT = KV_RANK + QK_ROPE
GM_GROUPS = 4
GM_CHUNK = 128
GM_GROUP_W = 128
GM_WIDTH = GM_GROUPS * GM_GROUP_W
R_HEADS = 8
R_HEAD_DIM = 64
R_WIDTH = R_HEADS * R_HEAD_DIM
D_DECAY = 64
D_AAA = 64
D_GATE = 128
R_SHIFT_W = 3 * R_WIDTH + D_DECAY + D_AAA + D_GATE
GN_EPS = 64e-5
D_FF = -(-8 * D_MODEL // (3 * 256)) * 256
ALPHA = (2 * DEPTH) ** 0.25
WKV_CHUNK = 64
V7X_VMEM_LIMIT = 56 * 1024 * 1024
LANES = 128


def _cparams(sem):
    return pltpu.CompilerParams(dimension_semantics=sem, vmem_limit_bytes=V7X_VMEM_LIMIT)


def _tile(n, pref):
    t = min(n, pref)
    while n % t:
        t //= 2
    return t


def _resident(a, layer=None):
    if layer is None:
        return pl.BlockSpec(a.shape, lambda *_: (0,) * a.ndim)
    return pl.BlockSpec((None,) + a.shape[1:], lambda *_: (layer,) + (0,) * (a.ndim - 1))


def _nt(a, b):
    return lax.dot_general(a, b, (((1,), (1,)), ((), ())), preferred_element_type=F32)


def _split3(x):
    hi = x.astype(BF16)
    r1 = x - hi.astype(F32)
    mid = r1.astype(BF16)
    lo = (r1 - mid.astype(F32)).astype(BF16)
    return hi, mid, lo


def _repeat(x, n, axis):
    return jnp.concatenate([x] * n, axis=axis)


def _tn(a, b):
    return lax.dot_general(a, b, (((0,), (0,)), ((), ())), preferred_element_type=F32)


def _exact_apply(f, x):
    hi, mid, lo = _split3(x)
    return (f(hi) + f(mid)) + f(lo)


def _sel_dot(x, e):
    return _exact_apply(lambda p: jnp.dot(p, e, preferred_element_type=F32), x)


def _sel_dot_left(e, x):
    return _exact_apply(lambda p: jnp.dot(e, p, preferred_element_type=F32), x)


def _layer_norm_rows(x, g, b, eps):
    xc = x - jnp.mean(x, -1, keepdims=True)
    var = jnp.mean(xc * xc, -1, keepdims=True)
    return xc * lax.rsqrt(var + eps) * g + b


def _mla_in_kernel(x_ref, w_ref, qg_ref, kvg_ref, cos64_ref, sin64_ref, cos512_ref, sin512_ref,
                   wqn_ref, wqr_ref, wqrr_ref, wuk_ref,
                   ckv_ref, kr_ref, kcat_ref, qcat_ref):
    x = x_ref[0].astype(BF16)
    p = _nt(x, w_ref[...])
    cq = p[:, :Q_RANK]
    cq = cq * lax.rsqrt(jnp.mean(cq * cq, -1, keepdims=True) + 1e-6) * qg_ref[...]
    ckv = p[:, Q_RANK:Q_RANK + KV_RANK]
    ckv = ckv * lax.rsqrt(jnp.mean(ckv * ckv, -1, keepdims=True) + 1e-6) * kvg_ref[...]
    kr = p[:, 768:832] * cos64_ref[0] + p[:, 896:960] * sin64_ref[0]
    ckv_ref[0] = ckv
    kr_ref[0] = kr
    kcat_ref[0, :, :KV_RANK] = ckv.astype(BF16)
    kcat_ref[0, :, KV_RANK:] = kr.astype(BF16)
    cqb = cq.astype(BF16)
    qn = _nt(cqb, wqn_ref[...])
    qr = _nt(cqb, wqr_ref[...])
    qrr = _nt(cqb, wqrr_ref[...])
    qrope = (qr * cos512_ref[0] + qrr * sin512_ref[0]) * Q_PRESCALE
    for h in range(MLA_HEADS):
        ql = _nt(qn[:, h * QK_NOPE:(h + 1) * QK_NOPE].astype(BF16), wuk_ref[h])
        qcat_ref[0, h, :, :KV_RANK] = (ql * Q_PRESCALE).astype(BF16)
        qcat_ref[0, h, :, KV_RANK:] = qrope[:, h * QK_ROPE:(h + 1) * QK_ROPE].astype(BF16)


def _mla_in(x, tabs, lw):
    B, T, D = x.shape
    tm = _tile(T, 512)
    cos64, sin64, cos512, sin512 = tabs
    row = lambda w: pl.BlockSpec((1, tm, w), lambda b, t: (b, t, 0))
    tab = lambda w: pl.BlockSpec((1, tm, w), lambda b, t: (0, t, 0))
    full = lambda a: _resident(a, lw['layer'])
    hd = lambda w: pl.BlockSpec((1, MLA_HEADS, tm, w), lambda b, t: (b, 0, t, 0))
    ws = (lw['w_mla_t'], lw['q_norm_g'], lw['kv_norm_g'])
    ws2 = (lw['w_uq_nope_t'], lw['w_uq_rope_t'], lw['w_uq_rope_rot_t'], lw['w_uk'])
    return pl.pallas_call(
        _mla_in_kernel,
        grid=(B, T // tm),
        in_specs=[row(D)] + [full(a) for a in ws] + [tab(64), tab(64), tab(512), tab(512)] + [full(a) for a in ws2],
        out_specs=[row(KV_RANK), row(QK_ROPE), row(QK_CAT), hd(QK_CAT)],
        out_shape=[jax.ShapeDtypeStruct((B, T, KV_RANK), F32), jax.ShapeDtypeStruct((B, T, QK_ROPE), F32),
                   jax.ShapeDtypeStruct((B, T, QK_CAT), BF16),
                   jax.ShapeDtypeStruct((B, MLA_HEADS, T, QK_CAT), BF16)],
        compiler_params=_cparams(("parallel", "parallel")),
        name="mla_in",
    )(x, *ws, cos64, sin64, cos512, sin512, *ws2)


ATTN_HEAD_GROUP = 2


def _attn_kernel(qi_ref, kj_ref, q_ref, k_ref, wuv_ref, o_ref, m_sc, l_sc, acc_sc, *, tq, tk):
    step = pl.program_id(1)
    qi = qi_ref[step]
    kj = kj_ref[step]
    H = MLA_HEADS
    hg = ATTN_HEAD_GROUP
    n_groups = H // hg
    R = hg * tq

    @pl.when(kj == 0)
    def _():
        m_sc[...] = jnp.full(m_sc.shape, -jnp.inf, F32)
        l_sc[...] = jnp.zeros(l_sc.shape, F32)
        acc_sc[...] = jnp.zeros(acc_sc.shape, F32)

    def update(masked):
        k = k_ref[0]
        kc = k_ref[0, :, :KV_RANK]
        if masked:
            q_pos = qi * tq + lax.broadcasted_iota(jnp.int32, (tq, tk), 0)
            k_pos = kj * tk + lax.broadcasted_iota(jnp.int32, (tq, tk), 1)
            visible = (k_pos <= q_pos)[None]

        def scores(g):
            return _nt(q_ref[0, g * hg:(g + 1) * hg].reshape(R, QK_CAT), k)

        def softmax(g, s):
            rows = slice(g * R, (g + 1) * R)
            if masked:
                s = jnp.where(visible, s.reshape(hg, tq, tk), -jnp.inf).reshape(R, tk)
            m_old = m_sc[rows]
            m_new = jnp.maximum(m_old, jnp.max(s, -1, keepdims=True))
            scale = jnp.exp2(m_old - m_new)
            p = jnp.exp2(s - _repeat(m_new, tk // LANES, axis=1))
            l_sc[rows] = scale * l_sc[rows] + jnp.sum(p, -1, keepdims=True)
            m_sc[rows] = m_new
            return p.astype(BF16), scale

        def values(g, p, scale):
            rows = slice(g * R, (g + 1) * R)
            acc_sc[rows] = (_repeat(scale, KV_RANK // LANES, axis=1) * acc_sc[rows]
                            + jnp.dot(p, kc, preferred_element_type=F32))

        s_next = scores(0)
        pending = None
        for g in range(n_groups):
            s_cur = s_next
            if g + 1 < n_groups:
                s_next = scores(g + 1)
            if pending is not None:
                values(*pending)
            pending = (g,) + softmax(g, s_cur)
        values(*pending)

    needs_mask = kj * tk + tk - 1 > qi * tq
    pl.when(needs_mask)(functools.partial(update, True))
    pl.when(jnp.logical_not(needs_mask))(functools.partial(update, False))

    @pl.when(kj == (qi * tq + tq - 1) // tk)
    def _():
        o = (acc_sc[...] / _repeat(l_sc[...], KV_RANK // LANES, axis=1)).astype(BF16)
        for h in range(H):
            o_ref[0, :, h * V_HEAD:(h + 1) * V_HEAD] = jnp.dot(
                o[h * tq:(h + 1) * tq], wuv_ref[h], preferred_element_type=F32).astype(o_ref.dtype)


def _attn_prompt(qcat, kcat, lw):
    B, H, T, _ = qcat.shape
    w_uv = lw['w_uv']
    tq = _tile(T, 256)
    tk = _tile(T, 512)
    pairs = [(i, j) for i in range(T // tq) for j in range((i * tq + tq - 1) // tk + 1)]
    qi_of = jnp.asarray([p[0] for p in pairs], jnp.int32)
    kj_of = jnp.asarray([p[1] for p in pairs], jnp.int32)
    grid_spec = pltpu.PrefetchScalarGridSpec(
        num_scalar_prefetch=2,
        grid=(B, len(pairs)),
        in_specs=[pl.BlockSpec((1, H, tq, QK_CAT), lambda b, p, qi, kj: (b, 0, qi[p], 0)),
                  pl.BlockSpec((1, tk, QK_CAT), lambda b, p, qi, kj: (b, kj[p], 0)),
                  _resident(w_uv, lw['layer'])],
        out_specs=pl.BlockSpec((1, tq, MLA_WIDTH), lambda b, p, qi, kj: (b, qi[p], 0)),
        scratch_shapes=[pltpu.VMEM((H * tq, LANES), F32), pltpu.VMEM((H * tq, LANES), F32),
                        pltpu.VMEM((H * tq, KV_RANK), F32)])
    return pl.pallas_call(
        functools.partial(_attn_kernel, tq=tq, tk=tk),
        grid_spec=grid_spec,
        out_shape=jax.ShapeDtypeStruct((B, T, MLA_WIDTH), BF16),
        compiler_params=_cparams(("parallel", "arbitrary")),
        name="attn_prompt",
    )(qi_of, kj_of, qcat, kcat, w_uv)


ATTN_SLOTS = 3


def _attn_sample_kernel(pt_ref, q_ref, cnew_ref, rnew_ref, wuv_ref, cc_hbm, cr_hbm, o_ref,
                        cbuf, rbuf, sem, s_sc, kb_sc, *, layer, n_pages, chunk):
    b = pl.program_id(0)
    nb = pl.num_programs(0)
    H = MLA_HEADS
    n_keys = n_pages * PAGE_SIZE

    n_chunks = n_keys // chunk
    pages_per_chunk = n_pages // n_chunks

    def copies(bb, slot, pages=range(n_pages)):
        out = []
        for j in pages:
            pg = pt_ref[bb, j]
            keys = pl.ds(j * PAGE_SIZE, PAGE_SIZE)
            out.append(pltpu.make_async_copy(cc_hbm.at[layer, pg], cbuf.at[slot, keys], sem.at[0, slot]))
            out.append(pltpu.make_async_copy(cr_hbm.at[layer, pg], rbuf.at[slot, :, keys], sem.at[1, slot]))
        return out

    @pl.when(b == 0)
    def _():
        for c in copies(0, 0) + copies(1, 1):
            c.start()

    slot = lax.rem(b, ATTN_SLOTS)
    slot_ahead = lax.rem(b + 2, ATTN_SLOTS)
    seq_ahead = jnp.minimum(b + 2, nb - 1)

    for c in copies(b, slot):
        c.wait()

    ql = q_ref[0, :, :KV_RANK]
    qr = q_ref[0, :, KV_RANK:]
    for c in range(n_chunks):
        keys = slice(c * chunk, (c + 1) * chunk)
        kc = cbuf[slot, keys].astype(BF16)
        kb_sc[keys] = kc
        s_rope = jnp.dot(qr, rbuf[slot, :, keys].astype(BF16), preferred_element_type=F32)
        s_sc[:, keys] = _nt(ql, kc) + s_rope
        for cp in copies(seq_ahead, slot_ahead, range(c * pages_per_chunk, (c + 1) * pages_per_chunk)):
            cp.start()
    cn = cnew_ref[0].astype(BF16)
    s_new = (jnp.sum(ql.astype(F32) * cn.astype(F32), -1, keepdims=True)
             + jnp.sum(qr.astype(F32) * rnew_ref[0].astype(BF16).astype(F32), -1, keepdims=True))
    s = s_sc[...]
    m = jnp.maximum(jnp.max(s, -1, keepdims=True), s_new)
    p = jnp.exp2(s - m)
    p_new = jnp.exp2(s_new - m)
    l = jnp.sum(p, -1, keepdims=True) + p_new
    p = (p / l).astype(BF16)
    acc = (p_new / l).astype(BF16).astype(F32) * cn.astype(F32)
    for c in range(n_keys // chunk):
        keys = slice(c * chunk, (c + 1) * chunk)
        acc = acc + jnp.dot(p[:, keys], kb_sc[keys], preferred_element_type=F32)
    ob = acc.astype(BF16)
    for h in range(H):
        o_ref[0, :, h * V_HEAD:(h + 1) * V_HEAD] = jnp.dot(
            ob[h:h + 1], wuv_ref[h], preferred_element_type=F32).astype(o_ref.dtype)

    @pl.when(b == nb - 1)
    def _():
        for c in copies(b, lax.rem(b + 1, ATTN_SLOTS)) + copies(b, slot_ahead):
            c.wait()


def _attn_sample(page_table, qcat, ckv_new, kr_new, lw, cache_c, cache_rt):
    B, n_pages = page_table.shape
    H = MLA_HEADS
    n_keys = n_pages * PAGE_SIZE
    chunk = _tile(n_keys, 1024)
    layer, w_uv = lw['layer'], lw['w_uv']
    assert B >= 2 and n_keys % chunk == 0 and chunk % PAGE_SIZE == 0
    grid_spec = pltpu.PrefetchScalarGridSpec(
        num_scalar_prefetch=1,
        grid=(B,),
        in_specs=[pl.BlockSpec((1, H, QK_CAT), lambda b, pt: (b, 0, 0)),
                  pl.BlockSpec((1, 1, KV_RANK), lambda b, pt: (b, 0, 0)),
                  pl.BlockSpec((1, 1, QK_ROPE), lambda b, pt: (b, 0, 0)),
                  _resident(w_uv, layer),
                  pl.BlockSpec(memory_space=pl.ANY),
                  pl.BlockSpec(memory_space=pl.ANY)],
        out_specs=pl.BlockSpec((1, 1, MLA_WIDTH), lambda b, pt: (b, 0, 0)),
        scratch_shapes=[pltpu.VMEM((ATTN_SLOTS, n_keys, KV_RANK), F32),
                        pltpu.VMEM((ATTN_SLOTS, QK_ROPE, n_keys), F32),
                        pltpu.SemaphoreType.DMA((2, ATTN_SLOTS)),
                        pltpu.VMEM((H, n_keys), F32),
                        pltpu.VMEM((n_keys, KV_RANK), BF16)])
    return pl.pallas_call(
        functools.partial(_attn_sample_kernel, layer=layer, n_pages=n_pages, chunk=chunk),
        grid_spec=grid_spec,
        out_shape=jax.ShapeDtypeStruct((B, 1, MLA_WIDTH), BF16),
        compiler_params=_cparams(("arbitrary",)),
        name="attn_sample",
    )(page_table, qcat, ckv_new, kr_new, w_uv, cache_c, cache_rt)


def _gelu_tanh(x):
    return 0.5 * x * (1.0 + jnp.tanh(0.7978845608028654 * (x + 0.044715 * (x * x * x))))


def _gm_kernel(x_ref, w_ref, lng_ref, lnb_ref, ws_ref, bs_ref, y_ref, *v_out, tm, single):
    x = x_ref[...].astype(BF16)
    z = _gelu_tanh(_nt(x, w_ref[...]))
    u = z[:, :GM_WIDTH]
    v = _layer_norm_rows(z[:, GM_WIDTH:], lng_ref[...], lnb_ref[...], 1e-5)
    if single:
        y_ref[...] = (u * (v * ws_ref[...] + bs_ref[...])).astype(y_ref.dtype)
        v_out[0][...] = v
        return
    t_i = lax.broadcasted_iota(jnp.int32, (GM_CHUNK, GM_CHUNK), 0)
    s_i = lax.broadcasted_iota(jnp.int32, (GM_CHUNK, GM_CHUNK), 1)
    wm = [jnp.where(s_i <= t_i, ws_ref[g], 0.0).astype(BF16) for g in range(GM_GROUPS)]
    vb = v.astype(BF16)
    for c in range(tm // GM_CHUNK):
        rows = slice(c * GM_CHUNK, (c + 1) * GM_CHUNK)
        for g in range(GM_GROUPS):
            cols = slice(g * GM_GROUP_W, (g + 1) * GM_GROUP_W)
            mixed = jnp.dot(wm[g], vb[rows, cols], preferred_element_type=F32) + bs_ref[:, cols]
            y_ref[rows, cols] = (u[rows, cols] * mixed).astype(y_ref.dtype)


def _gmlp(x2, lw, single):
    M, D = x2.shape
    tm = _tile(M, 256)
    ws, bs = (lw['gm_w00'], lw['gm_b0']) if single else (lw['gm_w_s'], lw['gm_b_full'])
    full = lambda a: _resident(a, lw['layer'])
    row = lambda w: pl.BlockSpec((tm, w), lambda i: (i, 0))
    out_specs = [row(GM_WIDTH)]
    out_shape = [jax.ShapeDtypeStruct((M, GM_WIDTH), BF16)]
    if single:
        out_specs.append(row(GM_WIDTH))
        out_shape.append(jax.ShapeDtypeStruct((M, GM_WIDTH), F32))
    ins = (lw['w_gm_t'], lw['gm_ln_g'], lw['gm_ln_b'], ws, bs)
    return pl.pallas_call(
        functools.partial(_gm_kernel, tm=tm, single=single),
        grid=(M // tm,),
        in_specs=[row(D)] + [full(a) for a in ins],
        out_specs=out_specs, out_shape=out_shape,
        compiler_params=_cparams(("parallel",)),
        name="gmlp",
    )(x2, *ins)


def _rw_in_kernel(x_ref, prev_ref, w_ref, mu_ref, wda_ref, w0a0_ref, g2_ref, kk_ref_w, ka_ref, rk_ref, e_ref,
                  *refs, tm, seq):
    outs, (g_o, bonus_o, shift_o, carry) = refs[:6], refs[6:]
    t = pl.program_id(1)
    x = x_ref[0].astype(BF16)
    p = _nt(x, w_ref[...])
    if seq:
        @pl.when(t == 0)
        def _():
            carry[...] = prev_ref[0]
        row = lax.broadcasted_iota(jnp.int32, (tm, 1), 0)
        prev = jnp.where(row == 0, carry[...], pltpu.roll(p, 1, axis=0))
        carry[...] = p[tm - 1:tm]
    else:
        prev = prev_ref[0]
    shift_o[0] = p[tm - 1:tm] if seq else p
    xm = p + (prev - p) * mu_ref[...]
    r = xm[:, :R_WIDTH]
    k = xm[:, R_WIDTH:2 * R_WIDTH]
    v = xm[:, 2 * R_WIDTH:3 * R_WIDTH]
    da = xm[:, 3 * R_WIDTH:3 * R_WIDTH + D_DECAY + D_AAA]
    lane = lax.broadcasted_iota(jnp.int32, da.shape, 1)
    da = jnp.where(lane < D_DECAY, jnp.tanh(da), da)
    m = jnp.dot(da.astype(BF16), wda_ref[...], preferred_element_type=F32) + w0a0_ref[...]
    wv = m[:, :R_WIDTH]
    w_log = -(jnp.maximum(-wv, 0.0) + jnp.log1p(jnp.exp(-jnp.abs(wv)))) - 0.5
    logdecay = -jnp.exp(w_log)
    a = jax.nn.sigmoid(m[:, R_WIDTH:])
    gd = xm[:, 3 * R_WIDTH + D_DECAY + D_AAA:]
    g = jnp.dot(jax.nn.sigmoid(gd).astype(BF16), g2_ref[...], preferred_element_type=F32)
    e = e_ref[...]
    kk = k * kk_ref_w[...]
    kk = kk * lax.rsqrt(jnp.maximum(_sel_dot(kk * kk, e), 1e-24))
    k2 = k * (1.0 + (a - 1.0) * ka_ref[...])
    bonus = _sel_dot(r * k2 * rk_ref[...], e) * v
    g_o[0] = g
    bonus_o[0] = bonus
    bb = kk * a
    if seq:
        C = WKV_CHUNK
        ti = lax.broadcasted_iota(jnp.int32, (tm, tm), 0)
        si = lax.broadcasted_iota(jnp.int32, (tm, tm), 1)
        tri = jnp.where((ti // C == si // C) & (si <= ti), 1.0, 0.0).astype(BF16)
        ci = lax.broadcasted_iota(jnp.int32, (tm // C, tm), 0)
        sj = lax.broadcasted_iota(jnp.int32, (tm // C, tm), 1)
        in_chunk = jnp.where(sj // C == ci, 1.0, 0.0).astype(BF16)
        cum = _sel_dot_left(tri, logdecay)
        p_in = jnp.exp(cum)
        p_inv = jnp.exp(-cum)
        pc_o = outs[5]
        pc_o[0] = jnp.exp(_sel_dot_left(in_chunk, logdecay))
        vals = (-kk * jnp.exp(cum - logdecay), bb * p_inv, k2 * p_inv, r * p_in, v)
        for h in range(R_HEADS):
            c = slice(h * R_HEAD_DIM, (h + 1) * R_HEAD_DIM)
            for o_ref, val in zip(outs, vals):
                o_ref[0, h] = val[:, c].astype(o_ref.dtype)
    else:
        for o_ref, val in zip(outs, (r, logdecay, k2, v, kk, bb)):
            o_ref[0] = val


def _rw_in(x, prev, lw, seq):
    B, T, D = x.shape
    tm = _tile(T, 512 if seq else 256)
    row = lambda w: pl.BlockSpec((1, tm, w), lambda b, t: (b, t, 0))
    hd = pl.BlockSpec((1, R_HEADS, tm, R_HEAD_DIM), lambda b, t: (b, 0, t, 0))
    ws = (lw['w_rw_t'], lw['rw_mu'], lw['rw_wda'], lw['rw_w0a0'], lw['rw_g2'], lw['rw_k_k'], lw['rw_k_a'],
          lw['rw_r_k'])
    w_specs = [_resident(a, lw['layer']) for a in ws] + [_resident(lw['head_ind'])]
    ws = ws + (lw['head_ind'],)
    hshape = lambda dt: jax.ShapeDtypeStruct((B, R_HEADS, T, R_HEAD_DIM), dt)
    if seq:
        C = WKV_CHUNK
        prev_spec = pl.BlockSpec((1, 1, R_SHIFT_W), lambda b, t: (b, 0, 0))
        shift_spec = pl.BlockSpec((1, 1, R_SHIFT_W), lambda b, t: (b, 0, 0))
        shift_shape = jax.ShapeDtypeStruct((B, 1, R_SHIFT_W), F32)
        head_specs = [hd] * 5 + [pl.BlockSpec((1, tm // C, R_WIDTH), lambda b, t: (b, t, 0))]
        head_shapes = [hshape(BF16)] * 5 + [jax.ShapeDtypeStruct((B, T // C, R_WIDTH), F32)]
    else:
        prev_spec = row(R_SHIFT_W)
        shift_spec = row(R_SHIFT_W)
        shift_shape = jax.ShapeDtypeStruct((B, T, R_SHIFT_W), F32)
        head_specs = [row(R_WIDTH)] * 6
        head_shapes = [jax.ShapeDtypeStruct((B, T, R_WIDTH), F32)] * 6
    fshape = jax.ShapeDtypeStruct((B, T, R_WIDTH), F32)
    return pl.pallas_call(
        functools.partial(_rw_in_kernel, tm=tm, seq=seq),
        grid=(B, T // tm),
        in_specs=[row(D), prev_spec] + w_specs,
        out_specs=head_specs + [row(R_WIDTH), row(R_WIDTH), shift_spec],
        out_shape=head_shapes + [fshape, fshape, shift_shape],
        scratch_shapes=[pltpu.VMEM((1, R_SHIFT_W), F32)],
        compiler_params=_cparams(("parallel", "arbitrary")),
        name="rw_in",
    )(x, prev, *ws)


def _bdot(a, b):
    return jnp.dot(a.astype(BF16), b.astype(BF16), preferred_element_type=F32)


def _bdot_nt(a, b):
    return _nt(a.astype(BF16), b.astype(BF16))


def _bdot_tn(a, b):
    return _tn(a.astype(BF16), b.astype(BF16))


def _wkv_kernel(a_ref, b_ref, k_ref, r_ref, v_ref, pc_ref, y_ref, s_ref, state):
    c = pl.program_id(0)
    n_seq, n_head, C, _ = a_ref.shape

    @pl.when(c == 0)
    def _():
        state[...] = jnp.zeros(state.shape, F32)

    ti = lax.broadcasted_iota(jnp.int32, (C, C), 0)
    si = lax.broadcasted_iota(jnp.int32, (C, C), 1)
    incl = si <= ti
    strict = si < ti
    n_steps = C.bit_length() - 1
    ch = [(b, h) for b in range(n_seq) for h in range(n_head)]
    each = lambda f, *lists: [f(*args) for args in zip(*lists)]
    A = [a_ref[b, h] for b, h in ch]
    Bm = [b_ref[b, h] for b, h in ch]
    K = [k_ref[b, h] for b, h in ch]
    R = [r_ref[b, h] for b, h in ch]
    V = [v_ref[b, h] for b, h in ch]
    Nb = each(lambda a, bm: jnp.where(strict, _nt(a, bm), 0.0).astype(BF16), A, Bm)
    Mm = each(lambda a, k: jnp.where(strict, _nt(a, k), 0.0), A, K)
    XA = [a.astype(F32) for a in A]
    XU = each(_bdot, Mm, V)
    for i in range(n_steps):
        XA = each(lambda n, x: x + _bdot(n, x), Nb, XA)
        XU = each(lambda n, x: x + _bdot(n, x), Nb, XU)
        if i + 1 < n_steps:
            Nb = each(lambda n: _bdot(n, n).astype(BF16), Nb)
    RB = each(lambda r, bm: jnp.where(incl, _nt(r, bm), 0.0).astype(BF16), R, Bm)
    RK = each(lambda r, k: jnp.where(incl, _nt(r, k), 0.0), R, K)
    Q = each(lambda r, rb, xa: r.astype(F32) + _bdot(rb, xa), R, RB, XA)
    Y0 = each(lambda rb, xu, rk, v: _bdot(rb, xu) + _bdot(rk, v), RB, XU, RK, V)
    G = each(_bdot_tn, XA, Bm)
    Hm = each(lambda xu, bm, v, k: _bdot_tn(xu, bm) + _bdot_tn(v, k), XU, Bm, V, K)
    S0 = [state[b, h] for b, h in ch]
    Y = each(lambda q, s0, y0: _bdot_nt(q, s0) + y0, Q, S0, Y0)
    pc = [pc_ref[b, 0, :, h * R_HEAD_DIM:(h + 1) * R_HEAD_DIM] for b, h in ch]
    S1 = each(lambda s0, g, hm, p: (s0 + _bdot(s0, g) + hm) * p, S0, G, Hm, pc)
    for (b, h), s1 in zip(ch, S1):
        state[b, h] = s1
        s_ref[b, h] = s1
    for b in range(n_seq):
        y_ref[b] = jnp.concatenate(Y[b * n_head:(b + 1) * n_head], axis=-1)


def _wkv_prompt(A, Bm, K, R, V, pc):
    B, H, T, N = A.shape
    C = WKV_CHUNK
    hd = pl.BlockSpec((B, H, C, N), lambda c: (0, 0, c, 0))
    return pl.pallas_call(
        _wkv_kernel,
        grid=(T // C,),
        in_specs=[hd] * 5 + [pl.BlockSpec((B, 1, 1, H * N), lambda c: (0, c, 0, 0))],
        out_specs=[pl.BlockSpec((B, C, H * N), lambda c: (0, c, 0)),
                   pl.BlockSpec((B, H, N, N), lambda c: (0, 0, 0, 0))],
        out_shape=[jax.ShapeDtypeStruct((B, T, H * N), F32), jax.ShapeDtypeStruct((B, H, N, N), F32)],
        scratch_shapes=[pltpu.VMEM((B, H, N, N), F32)],
        compiler_params=_cparams(("arbitrary",)),
        name="wkv_prompt",
    )(A, Bm, K, R, V, pc.reshape(B, T // C, 1, H * N))


def _wkv_step_kernel(r_ref, lw_ref, k_ref, v_ref, kk_ref, b_ref, s_ref, sel_ref, rep_ref, y_ref, so_ref):
    n = R_HEAD_DIM
    h = pl.program_id(0)
    sel = sel_ref[...]
    rep = rep_ref[...]
    head_t = lambda ref: _exact_apply(lambda p: _nt(sel, p), ref[0])
    over_i = lambda x: _repeat(x, n, axis=0)
    over_j = lambda x: _sel_dot_left(rep, x)
    sum_j = lambda x: _exact_apply(lambda p: _tn(rep, p), x)
    S = s_ref[...].reshape(n * n, s_ref.shape[-1])
    sa = sum_j(S * over_i(-head_t(kk_ref)))
    S1 = (S * over_i(jnp.exp(head_t(lw_ref))) + over_j(sa) * over_i(head_t(b_ref))
          + over_j(head_t(v_ref)) * over_i(head_t(k_ref)))
    so_ref[...] = S1.reshape(so_ref.shape)
    y_t = sum_j(S1 * over_i(head_t(r_ref)))

    @pl.when(h == 0)
    def _():
        y_ref[...] = jnp.zeros(y_ref.shape, F32)

    y_ref[...] += _exact_apply(lambda p: _tn(p, sel), y_t)


def _wkv_step(r, lw, k, v, kk, b, S_all, layer, consts):
    _, Bd, W = r.shape
    _, H, n, _, _ = S_all.shape
    sel, rep = consts
    vec = pl.BlockSpec((1, Bd, W), lambda h: (0, 0, 0))
    return pl.pallas_call(
        _wkv_step_kernel,
        grid=(H,),
        in_specs=[vec] * 6 + [pl.BlockSpec((None, None, n, n, Bd), lambda h: (layer, h, 0, 0, 0)),
                              pl.BlockSpec((n, W), lambda h: (h, 0)), _resident(rep)],
        out_specs=[pl.BlockSpec((Bd, W), lambda h: (0, 0)),
                   pl.BlockSpec((None, n, n, Bd), lambda h: (h, 0, 0, 0))],
        out_shape=[jax.ShapeDtypeStruct((Bd, W), F32), jax.ShapeDtypeStruct((H, n, n, Bd), F32)],
        compiler_params=_cparams(("arbitrary",)),
        name="wkv_step",
    )(r, lw, k, v, kk, b, S_all, sel, rep)


def _rw_post_kernel(y_ref, bonus_ref, g_ref, lng_ref, lnb_ref, e_ref, o_ref):
    e = e_ref[...]
    y = y_ref[...]
    mean = _sel_dot(y, e) * (1.0 / R_HEAD_DIM)
    yc = y - mean
    var = _sel_dot(yc * yc, e) * (1.0 / R_HEAD_DIM)
    yn = yc * lax.rsqrt(var + GN_EPS) * lng_ref[...] + lnb_ref[...]
    o_ref[...] = ((yn + bonus_ref[...]) * g_ref[...]).astype(o_ref.dtype)


def _rw_post(y, bonus, g, lw):
    M, W = y.shape
    tm = _tile(M, 512)
    row = pl.BlockSpec((tm, W), lambda i: (i, 0))
    ws = (lw['rw_ln_g'], lw['rw_ln_b'], lw['head_ind'])
    w_specs = [_resident(lw['rw_ln_g'], lw['layer']), _resident(lw['rw_ln_b'], lw['layer']),
               _resident(lw['head_ind'])]
    return pl.pallas_call(
        _rw_post_kernel,
        grid=(M // tm,),
        in_specs=[row, row, row] + w_specs,
        out_specs=row,
        out_shape=jax.ShapeDtypeStruct((M, W), BF16),
        compiler_params=_cparams(("parallel",)),
        name="rw_post",
    )(y, bonus, g, *ws)


def _out_ln_kernel(x_ref, ya_ref, yg_ref, yr_ref, w_ref, g_ref, b_ref, o_ref):
    h = jnp.dot(ya_ref[...], w_ref[:MLA_WIDTH], preferred_element_type=F32)
    h = h + jnp.dot(yg_ref[...], w_ref[MLA_WIDTH:MLA_WIDTH + GM_WIDTH], preferred_element_type=F32)
    h = h + jnp.dot(yr_ref[...], w_ref[MLA_WIDTH + GM_WIDTH:], preferred_element_type=F32)
    o_ref[...] = _layer_norm_rows(ALPHA * x_ref[...] + h, g_ref[...], b_ref[...], 1e-5)


def _out_ln(x2, y_mla, y_gm, y_rw, lw):
    M, D = x2.shape
    tm = _tile(M, 512)
    row = lambda w: pl.BlockSpec((tm, w), lambda i: (i, 0))
    full = lambda a: _resident(a, lw['layer'])
    ws = (lw['w_out'], lw['ln1_g'], lw['ln1_b'])
    return pl.pallas_call(
        _out_ln_kernel,
        grid=(M // tm,),
        in_specs=[row(D), row(MLA_WIDTH), row(GM_WIDTH), row(R_WIDTH)] + [full(a) for a in ws],
        out_specs=row(D),
        out_shape=jax.ShapeDtypeStruct((M, D), F32),
        compiler_params=_cparams(("parallel",)),
        name="out_ln",
    )(x2, y_mla, y_gm, y_rw, *ws)


def _ffn_kernel(x_ref, wg_ref, wu_ref, wo_ref, g_ref, b_ref, o_ref, xb_sc, acc_sc):
    j = pl.program_id(1)

    @pl.when(j == 0)
    def _():
        xb_sc[...] = x_ref[...].astype(BF16)
        acc_sc[...] = jnp.zeros(acc_sc.shape, F32)

    xb = xb_sc[...]
    gate = jnp.dot(xb, wg_ref[...], preferred_element_type=F32)
    up = jnp.dot(xb, wu_ref[...], preferred_element_type=F32)
    hid = (gate * jax.nn.sigmoid(gate) * up).astype(BF16)
    acc_sc[...] += jnp.dot(hid, wo_ref[...], preferred_element_type=F32)

    @pl.when(j == pl.num_programs(1) - 1)
    def _():
        o_ref[...] = _layer_norm_rows(ALPHA * x_ref[...] + acc_sc[...], g_ref[...], b_ref[...], 1e-5)


def _ffn(x2, lw):
    M, D = x2.shape
    tm = _tile(M, 512)
    tf = 512
    nf = D_FF // tf
    l = lw['layer']
    full = lambda a: _resident(a, l)
    return pl.pallas_call(
        _ffn_kernel,
        grid=(M // tm, nf),
        in_specs=[pl.BlockSpec((tm, D), lambda i, j: (i, 0)),
                  pl.BlockSpec((None, D, tf), lambda i, j: (l, 0, j)),
                  pl.BlockSpec((None, D, tf), lambda i, j: (l, 0, j + nf)),
                  pl.BlockSpec((None, tf, D), lambda i, j: (l, j, 0)),
                  full(lw['ln2_g']), full(lw['ln2_b'])],
        out_specs=pl.BlockSpec((tm, D), lambda i, j: (i, 0)),
        out_shape=jax.ShapeDtypeStruct((M, D), F32),
        scratch_shapes=[pltpu.VMEM((tm, D), BF16), pltpu.VMEM((tm, D), F32)],
        compiler_params=_cparams(("parallel", "arbitrary")),
        name="ffn",
    )(x2, lw['w_ffn_in'], lw['w_ffn_in'], lw['w_ffn_out'], lw['ln2_g'], lw['ln2_b'])


def _rot_half_rows(w):
    half = QK_ROPE // 2
    return jnp.concatenate([-w[..., half:, :], w[..., :half, :]], axis=-2)


def _rope_tables(pos):
    half = QK_ROPE // 2
    inv = ROPE_THETA ** (-jnp.arange(half, dtype=F32) / half)
    ang = pos.astype(F32)[:, None] * inv[None, :]
    cos = jnp.cos(ang)
    sin = jnp.sin(ang)
    cos64 = jnp.concatenate([cos, cos], -1)[None]
    sin64 = jnp.concatenate([sin, sin], -1)[None]
    return cos64, sin64, jnp.tile(cos64, (1, 1, MLA_HEADS)), jnp.tile(sin64, (1, 1, MLA_HEADS))


def _prep_weights(P):
    L = P['w_in'].shape[0]
    w_in_t = jnp.swapaxes(P['w_in'], 1, 2)
    D = w_in_t.shape[2]
    o_gm = Q_RANK + KV_RANK + QK_ROPE
    o_rw = o_gm + 2 * GM_WIDTH
    w_kr_t = w_in_t[:, Q_RANK + KV_RANK:o_gm]
    z64 = jnp.zeros((L, 64, D), F32)
    w_mla_t = jnp.concatenate([w_in_t[:, :Q_RANK + KV_RANK], w_kr_t, z64, _rot_half_rows(w_kr_t), z64], axis=1)
    w_uq_t = jnp.transpose(P['w_uq'], (0, 2, 3, 1))
    w_uq_rope_t = w_uq_t[:, :, QK_NOPE:]
    zda = jnp.zeros((L, D_DECAY, R_WIDTH), F32)
    wda = jnp.concatenate([jnp.concatenate([P['rw_w2'], zda], 2),
                           jnp.concatenate([zda, P['rw_a2']], 2)], 1)
    hid = jnp.arange(R_WIDTH) // R_HEAD_DIM
    row = lambda a: a.reshape(L, 1, -1)
    gm_w = P['gm_w_s']
    gm_b = P['gm_b_s']
    return {
        'w_mla_t': w_mla_t.astype(BF16),
        'q_norm_g': row(P['q_norm_g']), 'kv_norm_g': row(P['kv_norm_g']),
        'w_uq_nope_t': w_uq_t[:, :, :QK_NOPE].reshape(L, -1, Q_RANK).astype(BF16),
        'w_uq_rope_t': w_uq_rope_t.reshape(L, -1, Q_RANK).astype(BF16),
        'w_uq_rope_rot_t': _rot_half_rows(w_uq_rope_t).reshape(L, -1, Q_RANK).astype(BF16),
        'w_uk': P['w_uk'].astype(BF16),
        'w_uv': P['w_uv'].astype(BF16),
        'w_gm_t': w_in_t[:, o_gm:o_rw].astype(BF16),
        'gm_ln_g': row(P['gm_ln_g']), 'gm_ln_b': row(P['gm_ln_b']),
        'gm_w_s': gm_w,
        'gm_b_full': jnp.repeat(jnp.swapaxes(gm_b, 1, 2), GM_GROUP_W, axis=2),
        'gm_w00': row(jnp.repeat(gm_w[:, :, 0, 0], GM_GROUP_W, axis=1)),
        'gm_b0': row(jnp.repeat(gm_b[:, :, 0], GM_GROUP_W, axis=1)),
        'w_rw_t': w_in_t[:, o_rw:].astype(BF16),
        'rw_mu': row(P['rw_mu']),
        'rw_wda': wda.astype(BF16),
        'rw_w0a0': row(jnp.concatenate([P['rw_w0'], P['rw_a0']], axis=1)),
        'rw_g2': P['rw_g2'].astype(BF16),
        'rw_k_k': row(P['rw_k_k']), 'rw_k_a': row(P['rw_k_a']), 'rw_r_k': row(P['rw_r_k']),
        'rw_ln_g': row(P['rw_ln_g']), 'rw_ln_b': row(P['rw_ln_b']),
        'head_ind': (hid[:, None] == hid[None, :]).astype(BF16),
        'w_out': P['w_out'].astype(BF16),
        'ln1_g': row(P['ln1_g']), 'ln1_b': row(P['ln1_b']),
        'w_ffn_in': P['w_ffn_in'].astype(BF16),
        'w_ffn_out': P['w_ffn_out'].astype(BF16),
        'ln2_g': row(P['ln2_g']), 'ln2_b': row(P['ln2_b']),
    }


def _step_consts():
    n = R_HEAD_DIM
    sel = jnp.eye(R_WIDTH, dtype=BF16)
    rep = (jnp.arange(n * n)[:, None] // n == jnp.arange(n)[None, :]).astype(BF16)
    return sel, rep


def _layer_prompt(x, tabs, lw):
    B, T, D = x.shape
    ckv, kr, kcat, qcat = _mla_in(x, tabs, lw)
    y_mla = _attn_prompt(qcat, kcat, lw)
    x2 = x.reshape(B * T, D)
    y_gm, = _gmlp(x2, lw, single=False)
    shift0 = jnp.zeros((B, 1, R_SHIFT_W), F32)
    A, Bm, K, R, V, pc, g, bonus, shift = _rw_in(x, shift0, lw, seq=True)
    y, S = _wkv_prompt(A, Bm, K, R, V, pc)
    y_rw = _rw_post(y.reshape(B * T, R_WIDTH), bonus.reshape(B * T, R_WIDTH), g.reshape(B * T, R_WIDTH), lw)
    x2 = _out_ln(x2, y_mla.reshape(B * T, MLA_WIDTH), y_gm, y_rw, lw)
    x2 = _ffn(x2, lw)
    return x2.reshape(B, T, D), ckv, kr, S, shift.reshape(B, R_SHIFT_W)


def _layer_sample(x, tabs, lw, consts, page_table, cache_c, cache_r, shift0, S_all):
    _, Bd, D = x.shape
    ckv, kr, _, qcat = _mla_in(x, tabs, lw)
    tr = lambda a: jnp.swapaxes(a[0], 0, 1)
    y_mla = _attn_sample(page_table, tr(qcat), ckv.reshape(Bd, 1, KV_RANK),
                         kr.reshape(Bd, 1, QK_ROPE), lw, cache_c, cache_r)
    x2 = x.reshape(Bd, D)
    y_gm, v_gm = _gmlp(x2, lw, single=True)
    r, lgw, k, v, kk, b, g, bonus, shift = _rw_in(x, shift0[None], lw, seq=False)
    y, S = _wkv_step(r, lgw, k, v, kk, b, S_all, lw['layer'], consts)
    y_rw = _rw_post(y, bonus[0], g[0], lw)
    x2 = _out_ln(x2, y_mla.reshape(Bd, MLA_WIDTH), y_gm, y_rw, lw)
    x2 = _ffn(x2, lw)
    return (x2.reshape(1, Bd, D), ckv.reshape(Bd, 1, KV_RANK), kr.reshape(Bd, 1, QK_ROPE),
            S, shift[0], v_gm.reshape(Bd, 1, GM_WIDTH))


def kernel(x_prompt, x_sample, cache_c_kv, cache_k_rope, state_rwkv_wkv, state_rwkv_shift, page_table, w_in, q_norm_g, kv_norm_g, w_uq, w_uk, w_uv, gm_ln_g, gm_ln_b, gm_w_s, gm_b_s, rw_mu, rw_w0, rw_w2, rw_a0, rw_a2, rw_g2, rw_k_k, rw_k_a, rw_r_k, rw_ln_g, rw_ln_b, w_out, ln1_g, ln1_b, w_ffn_in, w_ffn_out, ln2_g, ln2_b):
    P = dict(w_in=w_in, q_norm_g=q_norm_g, kv_norm_g=kv_norm_g, w_uq=w_uq, w_uk=w_uk, w_uv=w_uv,
             gm_ln_g=gm_ln_g, gm_ln_b=gm_ln_b, gm_w_s=gm_w_s, gm_b_s=gm_b_s, rw_mu=rw_mu, rw_w0=rw_w0,
             rw_w2=rw_w2, rw_a0=rw_a0, rw_a2=rw_a2, rw_g2=rw_g2, rw_k_k=rw_k_k, rw_k_a=rw_k_a,
             rw_r_k=rw_r_k, rw_ln_g=rw_ln_g, rw_ln_b=rw_ln_b, w_out=w_out, ln1_g=ln1_g, ln1_b=ln1_b,
             w_ffn_in=w_ffn_in, w_ffn_out=w_ffn_out, ln2_g=ln2_g, ln2_b=ln2_b)
    n_dec, t_dec, _ = x_sample.shape
    assert t_dec == 1
    t_prompt = x_prompt.shape[1]
    past_len = page_table.shape[1] * PAGE_SIZE
    tabs_p = _rope_tables(jnp.arange(t_prompt, dtype=jnp.int32))
    tabs_s = _rope_tables(jnp.full((n_dec,), past_len, dtype=jnp.int32))
    consts = _step_consts()
    cache_k_rope_t = jnp.swapaxes(cache_k_rope, 2, 3)
    wkv_t = jnp.transpose(state_rwkv_wkv, (0, 2, 3, 4, 1))
    xp = x_prompt
    xs = x_sample.reshape(1, n_dec, -1)
    outs = [[] for _ in range(9)]
    W = _prep_weights(P)
    for l in range(w_in.shape[0]):
        lw = dict(W, layer=l)
        xp, ckv_p, kr_p, S_p, sh_p = _layer_prompt(xp, tabs_p, lw)
        xs, ckv_s, kr_s, S_s, sh_s, vg_s = _layer_sample(xs, tabs_s, lw, consts, page_table, cache_c_kv,
                                                         cache_k_rope_t, state_rwkv_shift[l], wkv_t)
        for lst, a in zip(outs, (ckv_p, kr_p, ckv_s, kr_s, S_p, sh_p, S_s, sh_s, vg_s)):
            lst.append(a)
    outs = [jnp.stack(o) for o in outs]
    outs[6] = jnp.transpose(outs[6], (0, 4, 1, 2, 3))
    return (xp, xs.reshape(n_dec, 1, -1)) + tuple(outs)
```

```python
import functools

import jax
import jax.numpy as jnp
from jax import lax
from jax.experimental import pallas as pl
from jax.experimental.pallas import tpu as pltpu

F32 = jnp.float32
BF16 = jnp.bfloat16

D_MODEL = 2048
DEPTH = 4
PAGE_SIZE = 128
MLA_HEADS = 8
QK_NOPE = 128
QK_ROPE = 64
V_HEAD = 128
Q_RANK = 512
KV_RANK = 256
ROPE_THETA = 10000.0
MLA_WIDTH = MLA_HEADS * V_HEAD
SM_SCALE = (QK_NOPE + QK_ROPE) ** -0.5
LOG2_E = 1.4426950408889634
Q_PRESCALE = SM_SCALE * LOG2_E
QK_CAT = KV_RANK + QK_ROPE
GM_GROUPS = 4
GM_CHUNK = 128
GM_GROUP_W = 128
GM_WIDTH = GM_GROUPS * GM_GROUP_W
R_HEADS = 8
R_HEAD_DIM = 64
R_WIDTH = R_HEADS * R_HEAD_DIM
D_DECAY = 64
D_AAA = 64
D_GATE = 128
R_SHIFT_W = 3 * R_WIDTH + D_DECAY + D_AAA + D_GATE
GN_EPS = 64e-5
D_FF = -(-8 * D_MODEL // (3 * 256)) * 256
ALPHA = (2 * DEPTH) ** 0.25
WKV_CHUNK = 64
V7X_VMEM_LIMIT = 56 * 1024 * 1024
V7X_VMEM_LIMIT_MAX = 60 * 1024 * 1024
LANES = 128


def _cparams(sem, vmem_limit=V7X_VMEM_LIMIT):
    return pltpu.CompilerParams(dimension_semantics=sem, vmem_limit_bytes=vmem_limit)


def _tile(n, pref):
    t = min(n, pref)
    while n % t:
        t //= 2
    return t


def _resident(a, layer=None):
    if layer is None:
        return pl.BlockSpec(a.shape, lambda *_: (0,) * a.ndim)
    return pl.BlockSpec((None,) + a.shape[1:], lambda *_: (layer,) + (0,) * (a.ndim - 1))


def _nt(a, b):
    return lax.dot_general(a, b, (((1,), (1,)), ((), ())), preferred_element_type=F32)


def _split3(x):
    hi = x.astype(BF16)
    r1 = x - hi.astype(F32)
    mid = r1.astype(BF16)
    lo = (r1 - mid.astype(F32)).astype(BF16)
    return hi, mid, lo


def _repeat(x, n, axis):
    return jnp.concatenate([x] * n, axis=axis)


def _tn(a, b):
    return lax.dot_general(a, b, (((0,), (0,)), ((), ())), preferred_element_type=F32)


def _exact_apply(f, x):
    hi, mid, lo = _split3(x)
    return (f(hi) + f(mid)) + f(lo)


def _sel_dot(x, e):
    return _exact_apply(lambda p: jnp.dot(p, e, preferred_element_type=F32), x)


def _sel_dot_left(e, x):
    return _exact_apply(lambda p: jnp.dot(e, p, preferred_element_type=F32), x)


def _layer_norm_rows(x, g, b, eps):
    xc = x - jnp.mean(x, -1, keepdims=True)
    var = jnp.mean(xc * xc, -1, keepdims=True)
    return xc * lax.rsqrt(var + eps) * g + b


def _mla_in_kernel(x_ref, w_ref, qg_ref, kvg_ref, cos64_ref, sin64_ref, cos512_ref, sin512_ref,
                   wqn_ref, wqr_ref, wqrr_ref, wuk_ref,
                   ckv_ref, kr_ref, kcat_ref, qcat_ref):
    x = x_ref[0].astype(BF16)
    p = _nt(x, w_ref[...])
    cq = p[:, :Q_RANK]
    cq = cq * lax.rsqrt(jnp.mean(cq * cq, -1, keepdims=True) + 1e-6) * qg_ref[...]
    ckv = p[:, Q_RANK:Q_RANK + KV_RANK]
    ckv = ckv * lax.rsqrt(jnp.mean(ckv * ckv, -1, keepdims=True) + 1e-6) * kvg_ref[...]
    kr = p[:, 768:832] * cos64_ref[0] + p[:, 896:960] * sin64_ref[0]
    ckv_ref[0] = ckv
    kr_ref[0] = kr
    kcat_ref[0, :, :KV_RANK] = ckv.astype(BF16)
    kcat_ref[0, :, KV_RANK:] = kr.astype(BF16)
    cqb = cq.astype(BF16)
    qn = _nt(cqb, wqn_ref[...])
    qr = _nt(cqb, wqr_ref[...])
    qrr = _nt(cqb, wqrr_ref[...])
    qrope = (qr * cos512_ref[0] + qrr * sin512_ref[0]) * Q_PRESCALE
    for h in range(MLA_HEADS):
        ql = _nt(qn[:, h * QK_NOPE:(h + 1) * QK_NOPE].astype(BF16), wuk_ref[h])
        qcat_ref[0, h, :, :KV_RANK] = (ql * Q_PRESCALE).astype(BF16)
        qcat_ref[0, h, :, KV_RANK:] = qrope[:, h * QK_ROPE:(h + 1) * QK_ROPE].astype(BF16)


def _mla_in(x, tabs, lw):
    B, T, D = x.shape
    tm = _tile(T, 512)
    cos64, sin64, cos512, sin512 = tabs
    row = lambda w: pl.BlockSpec((1, tm, w), lambda b, t: (b, t, 0))
    tab = lambda w: pl.BlockSpec((1, tm, w), lambda b, t: (0, t, 0))
    full = lambda a: _resident(a, lw['layer'])
    hd = lambda w: pl.BlockSpec((1, MLA_HEADS, tm, w), lambda b, t: (b, 0, t, 0))
    ws = (lw['w_mla_t'], lw['q_norm_g'], lw['kv_norm_g'])
    ws2 = (lw['w_uq_nope_t'], lw['w_uq_rope_t'], lw['w_uq_rope_rot_t'], lw['w_uk'])
    return pl.pallas_call(
        _mla_in_kernel,
        grid=(B, T // tm),
        in_specs=[row(D)] + [full(a) for a in ws] + [tab(64), tab(64), tab(512), tab(512)] + [full(a) for a in ws2],
        out_specs=[row(KV_RANK), row(QK_ROPE), row(QK_CAT), hd(QK_CAT)],
        out_shape=[jax.ShapeDtypeStruct((B, T, KV_RANK), F32), jax.ShapeDtypeStruct((B, T, QK_ROPE), F32),
                   jax.ShapeDtypeStruct((B, T, QK_CAT), BF16),
                   jax.ShapeDtypeStruct((B, MLA_HEADS, T, QK_CAT), BF16)],
        compiler_params=_cparams(("parallel", "parallel")),
        name="mla_in",
    )(x, *ws, cos64, sin64, cos512, sin512, *ws2)


ATTN_HEAD_GROUP = 2


def _attn_kernel(qi_ref, kj_ref, q_ref, k_ref, wuv_ref, o_ref, m_sc, l_sc, acc_sc, *, tq, tk):
    step = pl.program_id(1)
    qi = qi_ref[step]
    kj = kj_ref[step]
    H = MLA_HEADS
    hg = ATTN_HEAD_GROUP
    n_groups = H // hg
    R = hg * tq

    @pl.when(kj == 0)
    def _():
        m_sc[...] = jnp.full(m_sc.shape, -jnp.inf, F32)
        l_sc[...] = jnp.zeros(l_sc.shape, F32)
        acc_sc[...] = jnp.zeros(acc_sc.shape, F32)

    def update(masked):
        k = k_ref[0]
        kc = k_ref[0, :, :KV_RANK]
        if masked:
            q_pos = qi * tq + lax.broadcasted_iota(jnp.int32, (tq, tk), 0)
            k_pos = kj * tk + lax.broadcasted_iota(jnp.int32, (tq, tk), 1)
            visible = (k_pos <= q_pos)[None]

        def scores(g):
            return _nt(q_ref[0, g * hg:(g + 1) * hg].reshape(R, QK_CAT), k)

        def softmax(g, s):
            rows = slice(g * R, (g + 1) * R)
            if masked:
                s = jnp.where(visible, s.reshape(hg, tq, tk), -jnp.inf).reshape(R, tk)
            m_old = m_sc[rows]
            m_new = jnp.maximum(m_old, jnp.max(s, -1, keepdims=True))
            scale = jnp.exp2(m_old - m_new)
            p = jnp.exp2(s - _repeat(m_new, tk // LANES, axis=1))
            l_sc[rows] = scale * l_sc[rows] + jnp.sum(p, -1, keepdims=True)
            m_sc[rows] = m_new
            return p.astype(BF16), scale

        def values(g, p, scale):
            rows = slice(g * R, (g + 1) * R)
            acc_sc[rows] = (_repeat(scale, KV_RANK // LANES, axis=1) * acc_sc[rows]
                            + jnp.dot(p, kc, preferred_element_type=F32))

        s_next = scores(0)
        pending = None
        for g in range(n_groups):
            s_cur = s_next
            if g + 1 < n_groups:
                s_next = scores(g + 1)
            if pending is not None:
                values(*pending)
            pending = (g,) + softmax(g, s_cur)
        values(*pending)

    needs_mask = kj * tk + tk - 1 > qi * tq
    pl.when(needs_mask)(functools.partial(update, True))
    pl.when(jnp.logical_not(needs_mask))(functools.partial(update, False))

    @pl.when(kj == (qi * tq + tq - 1) // tk)
    def _():
        o = (acc_sc[...] / _repeat(l_sc[...], KV_RANK // LANES, axis=1)).astype(BF16)
        for h in range(H):
            o_ref[0, :, h * V_HEAD:(h + 1) * V_HEAD] = jnp.dot(
                o[h * tq:(h + 1) * tq], wuv_ref[h], preferred_element_type=F32).astype(o_ref.dtype)


def _attn_prompt(qcat, kcat, lw):
    B, H, T, _ = qcat.shape
    w_uv = lw['w_uv']
    tq = _tile(T, 256)
    tk = _tile(T, 512)
    pairs = [(i, j) for i in range(T // tq) for j in range((i * tq + tq - 1) // tk + 1)]
    qi_of = jnp.asarray([p[0] for p in pairs], jnp.int32)
    kj_of = jnp.asarray([p[1] for p in pairs], jnp.int32)
    grid_spec = pltpu.PrefetchScalarGridSpec(
        num_scalar_prefetch=2,
        grid=(B, len(pairs)),
        in_specs=[pl.BlockSpec((1, H, tq, QK_CAT), lambda b, p, qi, kj: (b, 0, qi[p], 0)),
                  pl.BlockSpec((1, tk, QK_CAT), lambda b, p, qi, kj: (b, kj[p], 0)),
                  _resident(w_uv, lw['layer'])],
        out_specs=pl.BlockSpec((1, tq, MLA_WIDTH), lambda b, p, qi, kj: (b, qi[p], 0)),
        scratch_shapes=[pltpu.VMEM((H * tq, LANES), F32), pltpu.VMEM((H * tq, LANES), F32),
                        pltpu.VMEM((H * tq, KV_RANK), F32)])
    return pl.pallas_call(
        functools.partial(_attn_kernel, tq=tq, tk=tk),
        grid_spec=grid_spec,
        out_shape=jax.ShapeDtypeStruct((B, T, MLA_WIDTH), BF16),
        compiler_params=_cparams(("parallel", "arbitrary")),
        name="attn_prompt",
    )(qi_of, kj_of, qcat, kcat, w_uv)


ATTN_SLOTS = 3


def _attn_sample_kernel(pt_ref, q_ref, cnew_ref, rnew_ref, wuv_ref, cc_hbm, cr_hbm, o_ref,
                        cbuf, rbuf, sem, s_sc, kb_sc, *, layer, n_pages, chunk):
    b = pl.program_id(0)
    nb = pl.num_programs(0)
    H = MLA_HEADS
    n_keys = n_pages * PAGE_SIZE

    n_chunks = n_keys // chunk
    pages_per_chunk = n_pages // n_chunks

    def copies(bb, slot, pages=range(n_pages)):
        out = []
        for j in pages:
            pg = pt_ref[bb, j]
            keys = pl.ds(j * PAGE_SIZE, PAGE_SIZE)
            out.append(pltpu.make_async_copy(cc_hbm.at[layer, pg], cbuf.at[slot, keys], sem.at[0, slot]))
            out.append(pltpu.make_async_copy(cr_hbm.at[layer, pg], rbuf.at[slot, j], sem.at[1, slot]))
        return out

    @pl.when(b == 0)
    def _():
        for c in copies(0, 0) + copies(1, 1):
            c.start()

    slot = lax.rem(b, ATTN_SLOTS)
    slot_ahead = lax.rem(b + 2, ATTN_SLOTS)
    seq_ahead = jnp.minimum(b + 2, nb - 1)

    for c in copies(b, slot):
        c.wait()

    ql = q_ref[0, :, :KV_RANK]
    qr = q_ref[0, :, KV_RANK:]
    for c in range(n_chunks):
        keys = slice(c * chunk, (c + 1) * chunk)
        kc = cbuf[slot, keys].astype(BF16)
        kb_sc[keys] = kc
        pages = range(c * pages_per_chunk, (c + 1) * pages_per_chunk)
        kr_t = jnp.concatenate([rbuf[slot, j] for j in pages], axis=1).astype(BF16)
        s_sc[:, keys] = _nt(ql, kc) + jnp.dot(qr, kr_t, preferred_element_type=F32)
        for cp in copies(seq_ahead, slot_ahead, range(c * pages_per_chunk, (c + 1) * pages_per_chunk)):
            cp.start()
    cn = cnew_ref[0].astype(BF16)
    s_new = (jnp.sum(ql.astype(F32) * cn.astype(F32), -1, keepdims=True)
             + jnp.sum(qr.astype(F32) * rnew_ref[0].astype(BF16).astype(F32), -1, keepdims=True))
    s = s_sc[...]
    m = jnp.maximum(jnp.max(s, -1, keepdims=True), s_new)
    p = jnp.exp2(s - m)
    p_new = jnp.exp2(s_new - m)
    l = jnp.sum(p, -1, keepdims=True) + p_new
    p = (p / l).astype(BF16)
    acc = (p_new / l).astype(BF16).astype(F32) * cn.astype(F32)
    for c in range(n_keys // chunk):
        keys = slice(c * chunk, (c + 1) * chunk)
        acc = acc + jnp.dot(p[:, keys], kb_sc[keys], preferred_element_type=F32)
    ob = acc.astype(BF16)
    for h in range(H):
        o_ref[0, :, h * V_HEAD:(h + 1) * V_HEAD] = jnp.dot(
            ob[h:h + 1], wuv_ref[h], preferred_element_type=F32).astype(o_ref.dtype)

    @pl.when(b == nb - 1)
    def _():
        for c in copies(b, lax.rem(b + 1, ATTN_SLOTS)) + copies(b, slot_ahead):
            c.wait()


def _attn_sample(page_table, qcat, ckv_new, kr_new, lw, cache_c, cache_rt):
    B, n_pages = page_table.shape
    H = MLA_HEADS
    n_keys = n_pages * PAGE_SIZE
    chunk = _tile(n_keys, 1024)
    layer, w_uv = lw['layer'], lw['w_uv']
    assert B >= 2 and n_keys % chunk == 0 and chunk % PAGE_SIZE == 0
    grid_spec = pltpu.PrefetchScalarGridSpec(
        num_scalar_prefetch=1,
        grid=(B,),
        in_specs=[pl.BlockSpec((1, H, QK_CAT), lambda b, pt: (b, 0, 0)),
                  pl.BlockSpec((1, 1, KV_RANK), lambda b, pt: (b, 0, 0)),
                  pl.BlockSpec((1, 1, QK_ROPE), lambda b, pt: (b, 0, 0)),
                  _resident(w_uv, layer),
                  pl.BlockSpec(memory_space=pl.ANY),
                  pl.BlockSpec(memory_space=pl.ANY)],
        out_specs=pl.BlockSpec((1, 1, MLA_WIDTH), lambda b, pt: (b, 0, 0)),
        scratch_shapes=[pltpu.VMEM((ATTN_SLOTS, n_keys, KV_RANK), F32),
                        pltpu.VMEM((ATTN_SLOTS, n_pages, QK_ROPE, PAGE_SIZE), F32),
                        pltpu.SemaphoreType.DMA((2, ATTN_SLOTS)),
                        pltpu.VMEM((H, n_keys), F32),
                        pltpu.VMEM((n_keys, KV_RANK), BF16)])
    return pl.pallas_call(
        functools.partial(_attn_sample_kernel, layer=layer, n_pages=n_pages, chunk=chunk),
        grid_spec=grid_spec,
        out_shape=jax.ShapeDtypeStruct((B, 1, MLA_WIDTH), BF16),
        compiler_params=_cparams(("arbitrary",)),
        name="attn_sample",
    )(page_table, qcat, ckv_new, kr_new, w_uv, cache_c, cache_rt)


def _gelu_tanh(x):
    return 0.5 * x * (1.0 + jnp.tanh(0.7978845608028654 * (x + 0.044715 * (x * x * x))))


def _gm_kernel(x_ref, w_ref, lng_ref, lnb_ref, ws_ref, bs_ref, y_ref, *v_out, tm, single):
    x = x_ref[...].astype(BF16)
    z = _gelu_tanh(_nt(x, w_ref[...]))
    u = z[:, :GM_WIDTH]
    v = _layer_norm_rows(z[:, GM_WIDTH:], lng_ref[...], lnb_ref[...], 1e-5)
    if single:
        y_ref[...] = (u * (v * ws_ref[...] + bs_ref[...])).astype(y_ref.dtype)
        v_out[0][...] = v
        return
    t_i = lax.broadcasted_iota(jnp.int32, (GM_CHUNK, GM_CHUNK), 0)
    s_i = lax.broadcasted_iota(jnp.int32, (GM_CHUNK, GM_CHUNK), 1)
    wm = [jnp.where(s_i <= t_i, ws_ref[g], 0.0).astype(BF16) for g in range(GM_GROUPS)]
    vb = v.astype(BF16)
    for c in range(tm // GM_CHUNK):
        rows = slice(c * GM_CHUNK, (c + 1) * GM_CHUNK)
        for g in range(GM_GROUPS):
            cols = slice(g * GM_GROUP_W, (g + 1) * GM_GROUP_W)
            mixed = jnp.dot(wm[g], vb[rows, cols], preferred_element_type=F32) + bs_ref[:, cols]
            y_ref[rows, cols] = (u[rows, cols] * mixed).astype(y_ref.dtype)


def _gmlp(x2, lw, single):
    M, D = x2.shape
    tm = _tile(M, 256)
    ws, bs = (lw['gm_w00'], lw['gm_b0']) if single else (lw['gm_w_s'], lw['gm_b_full'])
    full = lambda a: _resident(a, lw['layer'])
    row = lambda w: pl.BlockSpec((tm, w), lambda i: (i, 0))
    out_specs = [row(GM_WIDTH)]
    out_shape = [jax.ShapeDtypeStruct((M, GM_WIDTH), BF16)]
    if single:
        out_specs.append(row(GM_WIDTH))
        out_shape.append(jax.ShapeDtypeStruct((M, GM_WIDTH), F32))
    ins = (lw['w_gm_t'], lw['gm_ln_g'], lw['gm_ln_b'], ws, bs)
    return pl.pallas_call(
        functools.partial(_gm_kernel, tm=tm, single=single),
        grid=(M // tm,),
        in_specs=[row(D)] + [full(a) for a in ins],
        out_specs=out_specs, out_shape=out_shape,
        compiler_params=_cparams(("parallel",)),
        name="gmlp",
    )(x2, *ins)


def _rw_in_kernel(x_ref, prev_ref, w_ref, mu_ref, wda_ref, w0a0_ref, g2_ref, kk_ref_w, ka_ref, rk_ref, e_ref,
                  *refs, tm, seq):
    outs, (g_o, bonus_o, shift_o, carry) = refs[:6], refs[6:]
    t = pl.program_id(1)
    x = x_ref[0].astype(BF16)
    p = _nt(x, w_ref[...])
    if seq:
        @pl.when(t == 0)
        def _():
            carry[...] = prev_ref[0]
        row = lax.broadcasted_iota(jnp.int32, (tm, 1), 0)
        prev = jnp.where(row == 0, carry[...], pltpu.roll(p, 1, axis=0))
        carry[...] = p[tm - 1:tm]
    else:
        prev = prev_ref[0]
    shift_o[0] = p[tm - 1:tm] if seq else p
    xm = p + (prev - p) * mu_ref[...]
    r = xm[:, :R_WIDTH]
    k = xm[:, R_WIDTH:2 * R_WIDTH]
    v = xm[:, 2 * R_WIDTH:3 * R_WIDTH]
    da = xm[:, 3 * R_WIDTH:3 * R_WIDTH + D_DECAY + D_AAA]
    lane = lax.broadcasted_iota(jnp.int32, da.shape, 1)
    da = jnp.where(lane < D_DECAY, jnp.tanh(da), da)
    m = jnp.dot(da.astype(BF16), wda_ref[...], preferred_element_type=F32) + w0a0_ref[...]
    wv = m[:, :R_WIDTH]
    w_log = -(jnp.maximum(-wv, 0.0) + jnp.log1p(jnp.exp(-jnp.abs(wv)))) - 0.5
    logdecay = -jnp.exp(w_log)
    a = jax.nn.sigmoid(m[:, R_WIDTH:])
    gd = xm[:, 3 * R_WIDTH + D_DECAY + D_AAA:]
    g = jnp.dot(jax.nn.sigmoid(gd).astype(BF16), g2_ref[...], preferred_element_type=F32)
    e = e_ref[...]
    kk = k * kk_ref_w[...]
    kk = kk * lax.rsqrt(jnp.maximum(_sel_dot(kk * kk, e), 1e-24))
    k2 = k * (1.0 + (a - 1.0) * ka_ref[...])
    bonus = _sel_dot(r * k2 * rk_ref[...], e) * v
    g_o[0] = g
    bonus_o[0] = bonus
    bb = kk * a
    if seq:
        C = WKV_CHUNK
        ti = lax.broadcasted_iota(jnp.int32, (tm, tm), 0)
        si = lax.broadcasted_iota(jnp.int32, (tm, tm), 1)
        tri = jnp.where((ti // C == si // C) & (si <= ti), 1.0, 0.0).astype(BF16)
        ci = lax.broadcasted_iota(jnp.int32, (tm // C, tm), 0)
        sj = lax.broadcasted_iota(jnp.int32, (tm // C, tm), 1)
        in_chunk = jnp.where(sj // C == ci, 1.0, 0.0).astype(BF16)
        cum = _sel_dot_left(tri, logdecay)
        p_in = jnp.exp(cum)
        p_inv = jnp.exp(-cum)
        pc_o = outs[5]
        pc_o[0] = jnp.exp(_sel_dot_left(in_chunk, logdecay))
        vals = (-kk * jnp.exp(cum - logdecay), bb * p_inv, k2 * p_inv, r * p_in, v)
        for h in range(R_HEADS):
            c = slice(h * R_HEAD_DIM, (h + 1) * R_HEAD_DIM)
            for o_ref, val in zip(outs, vals):
                o_ref[0, h] = val[:, c].astype(o_ref.dtype)
    else:
        for o_ref, val in zip(outs, (r, logdecay, k2, v, kk, bb)):
            o_ref[0] = val


def _rw_in(x, prev, lw, seq):
    B, T, D = x.shape
    tm = _tile(T, 512 if seq else 256)
    row = lambda w: pl.BlockSpec((1, tm, w), lambda b, t: (b, t, 0))
    hd = pl.BlockSpec((1, R_HEADS, tm, R_HEAD_DIM), lambda b, t: (b, 0, t, 0))
    ws = (lw['w_rw_t'], lw['rw_mu'], lw['rw_wda'], lw['rw_w0a0'], lw['rw_g2'], lw['rw_k_k'], lw['rw_k_a'],
          lw['rw_r_k'])
    w_specs = [_resident(a, lw['layer']) for a in ws] + [_resident(lw['head_ind'])]
    ws = ws + (lw['head_ind'],)
    hshape = lambda dt: jax.ShapeDtypeStruct((B, R_HEADS, T, R_HEAD_DIM), dt)
    if seq:
        C = WKV_CHUNK
        prev_spec = pl.BlockSpec((1, 1, R_SHIFT_W), lambda b, t: (b, 0, 0))
        shift_spec = pl.BlockSpec((1, 1, R_SHIFT_W), lambda b, t: (b, 0, 0))
        shift_shape = jax.ShapeDtypeStruct((B, 1, R_SHIFT_W), F32)
        head_specs = [hd] * 5 + [pl.BlockSpec((1, tm // C, R_WIDTH), lambda b, t: (b, t, 0))]
        head_shapes = [hshape(BF16)] * 5 + [jax.ShapeDtypeStruct((B, T // C, R_WIDTH), F32)]
    else:
        prev_spec = row(R_SHIFT_W)
        shift_spec = row(R_SHIFT_W)
        shift_shape = jax.ShapeDtypeStruct((B, T, R_SHIFT_W), F32)
        head_specs = [row(R_WIDTH)] * 6
        head_shapes = [jax.ShapeDtypeStruct((B, T, R_WIDTH), F32)] * 6
    fshape = jax.ShapeDtypeStruct((B, T, R_WIDTH), F32)
    return pl.pallas_call(
        functools.partial(_rw_in_kernel, tm=tm, seq=seq),
        grid=(B, T // tm),
        in_specs=[row(D), prev_spec] + w_specs,
        out_specs=head_specs + [row(R_WIDTH), row(R_WIDTH), shift_spec],
        out_shape=head_shapes + [fshape, fshape, shift_shape],
        scratch_shapes=[pltpu.VMEM((1, R_SHIFT_W), F32)],
        compiler_params=_cparams(("parallel", "arbitrary")),
        name="rw_in",
    )(x, prev, *ws)


def _bdot(a, b):
    return jnp.dot(a.astype(BF16), b.astype(BF16), preferred_element_type=F32)


def _bdot_nt(a, b):
    return _nt(a.astype(BF16), b.astype(BF16))


def _bdot_tn(a, b):
    return _tn(a.astype(BF16), b.astype(BF16))


def _wkv_kernel(a_ref, b_ref, k_ref, r_ref, v_ref, pc_ref, y_ref, s_ref, state):
    c = pl.program_id(0)
    n_seq, n_head, C, _ = a_ref.shape

    @pl.when(c == 0)
    def _():
        state[...] = jnp.zeros(state.shape, F32)

    ti = lax.broadcasted_iota(jnp.int32, (C, 2 * C), 0)
    si = lax.broadcasted_iota(jnp.int32, (C, 2 * C), 1) % C
    incl2 = si <= ti
    strict2 = si < ti
    n_steps = C.bit_length() - 1
    ch = [(b, h) for b in range(n_seq) for h in range(n_head)]
    each = lambda f, *lists: [f(*args) for args in zip(*lists)]
    A = [a_ref[b, h] for b, h in ch]
    Bm = [b_ref[b, h] for b, h in ch]
    K = [k_ref[b, h] for b, h in ch]
    R = [r_ref[b, h] for b, h in ch]
    V = [v_ref[b, h] for b, h in ch]
    D = R_HEAD_DIM
    BK = each(lambda bm, k: jnp.concatenate([bm, k], axis=0), Bm, K)
    NM = each(lambda a, bk: jnp.where(strict2, _nt(a, bk), 0.0), A, BK)
    Nb = [nm[:, :C].astype(BF16) for nm in NM]
    XU = each(lambda nm, v: _bdot(nm[:, C:], v), NM, V)
    X = each(lambda a, xu: jnp.concatenate([a.astype(F32), xu], axis=1), A, XU)
    for i in range(n_steps):
        X = each(lambda n, x: x + _bdot(n, x), Nb, X)
        if i + 1 < n_steps:
            Nb = each(lambda n: _bdot(n, n).astype(BF16), Nb)
    RBK = each(lambda r, bk: jnp.where(incl2, _nt(r, bk), 0.0).astype(BF16), R, BK)
    RBX = each(lambda rbk, x: _bdot(rbk[:, :C], x), RBK, X)
    Q = each(lambda r, rbx: r.astype(F32) + rbx[:, :D], R, RBX)
    Y0 = each(lambda rbx, rbk, v: rbx[:, D:] + _bdot(rbk[:, C:], v), RBX, RBK, V)
    GH = each(_bdot_tn, X, Bm)
    G = [gh[:D] for gh in GH]
    Hm = each(lambda gh, v, k: gh[D:] + _bdot_tn(v, k), GH, V, K)
    S0 = [state[b, h] for b, h in ch]
    Y = each(lambda q, s0, y0: _bdot_nt(q, s0) + y0, Q, S0, Y0)
    pc = [pc_ref[b, 0, :, h * R_HEAD_DIM:(h + 1) * R_HEAD_DIM] for b, h in ch]
    S1 = each(lambda s0, g, hm, p: (s0 + _bdot(s0, g) + hm) * p, S0, G, Hm, pc)
    for (b, h), s1 in zip(ch, S1):
        state[b, h] = s1
        s_ref[b, h] = s1
    for b in range(n_seq):
        y_ref[b] = jnp.concatenate(Y[b * n_head:(b + 1) * n_head], axis=-1)


def _wkv_prompt(A, Bm, K, R, V, pc):
    B, H, T, N = A.shape
    C = WKV_CHUNK
    hd = pl.BlockSpec((B, H, C, N), lambda c: (0, 0, c, 0))
    return pl.pallas_call(
        _wkv_kernel,
        grid=(T // C,),
        in_specs=[hd] * 5 + [pl.BlockSpec((B, 1, 1, H * N), lambda c: (0, c, 0, 0))],
        out_specs=[pl.BlockSpec((B, C, H * N), lambda c: (0, c, 0)),
                   pl.BlockSpec((B, H, N, N), lambda c: (0, 0, 0, 0))],
        out_shape=[jax.ShapeDtypeStruct((B, T, H * N), F32), jax.ShapeDtypeStruct((B, H, N, N), F32)],
        scratch_shapes=[pltpu.VMEM((B, H, N, N), F32)],
        compiler_params=_cparams(("arbitrary",)),
        name="wkv_prompt",
    )(A, Bm, K, R, V, pc.reshape(B, T // C, 1, H * N))


def _wkv_step_kernel(r_ref, lw_ref, k_ref, v_ref, kk_ref, b_ref, s_ref, sel_ref, rep_ref, y_ref, so_ref):
    n = R_HEAD_DIM
    h = pl.program_id(0)
    sel = sel_ref[...]
    rep = rep_ref[...]
    head_t = lambda ref: _exact_apply(lambda p: _nt(sel, p), ref[0])
    over_i = lambda x: _repeat(x, n, axis=0)
    over_j = lambda x: _sel_dot_left(rep, x)
    sum_j = lambda x: _exact_apply(lambda p: _tn(rep, p), x)
    S = s_ref[...].reshape(n * n, s_ref.shape[-1])
    sa = sum_j(S * over_i(-head_t(kk_ref)))
    S1 = (S * over_i(jnp.exp(head_t(lw_ref))) + over_j(sa) * over_i(head_t(b_ref))
          + over_j(head_t(v_ref)) * over_i(head_t(k_ref)))
    so_ref[...] = S1.reshape(so_ref.shape)
    y_t = sum_j(S1 * over_i(head_t(r_ref)))

    @pl.when(h == 0)
    def _():
        y_ref[...] = jnp.zeros(y_ref.shape, F32)

    y_ref[...] += _exact_apply(lambda p: _tn(p, sel), y_t)


def _wkv_step(r, lw, k, v, kk, b, S_all, layer, consts):
    _, Bd, W = r.shape
    _, H, n, _, _ = S_all.shape
    sel, rep = consts
    vec = pl.BlockSpec((1, Bd, W), lambda h: (0, 0, 0))
    return pl.pallas_call(
        _wkv_step_kernel,
        grid=(H,),
        in_specs=[vec] * 6 + [pl.BlockSpec((None, None, n, n, Bd), lambda h: (layer, h, 0, 0, 0)),
                              pl.BlockSpec((n, W), lambda h: (h, 0)), _resident(rep)],
        out_specs=[pl.BlockSpec((Bd, W), lambda h: (0, 0)),
                   pl.BlockSpec((None, n, n, Bd), lambda h: (h, 0, 0, 0))],
        out_shape=[jax.ShapeDtypeStruct((Bd, W), F32), jax.ShapeDtypeStruct((H, n, n, Bd), F32)],
        compiler_params=_cparams(("arbitrary",)),
        name="wkv_step",
    )(r, lw, k, v, kk, b, S_all, sel, rep)


def _rw_post_kernel(y_ref, bonus_ref, g_ref, lng_ref, lnb_ref, e_ref, o_ref):
    e = e_ref[...]
    y = y_ref[...]
    mean = _sel_dot(y, e) * (1.0 / R_HEAD_DIM)
    yc = y - mean
    var = _sel_dot(yc * yc, e) * (1.0 / R_HEAD_DIM)
    yn = yc * lax.rsqrt(var + GN_EPS) * lng_ref[...] + lnb_ref[...]
    o_ref[...] = ((yn + bonus_ref[...]) * g_ref[...]).astype(o_ref.dtype)


def _rw_post(y, bonus, g, lw):
    M, W = y.shape
    tm = _tile(M, 512)
    row = pl.BlockSpec((tm, W), lambda i: (i, 0))
    ws = (lw['rw_ln_g'], lw['rw_ln_b'], lw['head_ind'])
    w_specs = [_resident(lw['rw_ln_g'], lw['layer']), _resident(lw['rw_ln_b'], lw['layer']),
               _resident(lw['head_ind'])]
    return pl.pallas_call(
        _rw_post_kernel,
        grid=(M // tm,),
        in_specs=[row, row, row] + w_specs,
        out_specs=row,
        out_shape=jax.ShapeDtypeStruct((M, W), BF16),
        compiler_params=_cparams(("parallel",)),
        name="rw_post",
    )(y, bonus, g, *ws)


def _out_ln_kernel(x_ref, ya_ref, yg_ref, yr_ref, w_ref, g_ref, b_ref, o_ref, ob_ref):
    h = jnp.dot(ya_ref[...], w_ref[:MLA_WIDTH], preferred_element_type=F32)
    h = h + jnp.dot(yg_ref[...], w_ref[MLA_WIDTH:MLA_WIDTH + GM_WIDTH], preferred_element_type=F32)
    h = h + jnp.dot(yr_ref[...], w_ref[MLA_WIDTH + GM_WIDTH:], preferred_element_type=F32)
    y = _layer_norm_rows(ALPHA * x_ref[...] + h, g_ref[...], b_ref[...], 1e-5)
    o_ref[...] = y
    ob_ref[...] = y.astype(BF16)


def _out_ln(x2, y_mla, y_gm, y_rw, lw):
    M, D = x2.shape
    tm = _tile(M, 512)
    row = lambda w: pl.BlockSpec((tm, w), lambda i: (i, 0))
    full = lambda a: _resident(a, lw['layer'])
    ws = (lw['w_out'], lw['ln1_g'], lw['ln1_b'])
    return pl.pallas_call(
        _out_ln_kernel,
        grid=(M // tm,),
        in_specs=[row(D), row(MLA_WIDTH), row(GM_WIDTH), row(R_WIDTH)] + [full(a) for a in ws],
        out_specs=[row(D), row(D)],
        out_shape=[jax.ShapeDtypeStruct((M, D), F32), jax.ShapeDtypeStruct((M, D), BF16)],
        compiler_params=_cparams(("parallel",)),
        name="out_ln",
    )(x2, y_mla, y_gm, y_rw, *ws)


def _ffn_up_kernel(xb_ref, wg_ref, wu_ref, h_ref):
    xb = xb_ref[...]
    gate = jnp.dot(xb, wg_ref[...], preferred_element_type=F32)
    up = jnp.dot(xb, wu_ref[...], preferred_element_type=F32)
    h_ref[...] = (gate * jax.nn.sigmoid(gate) * up).astype(h_ref.dtype)


def _ffn_down_kernel(x_ref, h_ref, wo_ref, g_ref, b_ref, o_ref, ob_ref):
    y = jnp.dot(h_ref[...], wo_ref[...], preferred_element_type=F32)
    y = _layer_norm_rows(ALPHA * x_ref[...] + y, g_ref[...], b_ref[...], 1e-5)
    o_ref[...] = y
    ob_ref[...] = y.astype(BF16)


def _ffn(x2, xb2, lw):
    M, D = x2.shape
    l = lw['layer']
    tm = _tile(M, 1024)
    tf = 512
    nf = D_FF // tf
    hid = pl.pallas_call(
        _ffn_up_kernel,
        grid=(M // tm, nf),
        in_specs=[pl.BlockSpec((tm, D), lambda i, j: (i, 0)),
                  pl.BlockSpec((None, D, tf), lambda i, j: (l, 0, j)),
                  pl.BlockSpec((None, D, tf), lambda i, j: (l, 0, j + nf))],
        out_specs=pl.BlockSpec((tm, tf), lambda i, j: (i, j)),
        out_shape=jax.ShapeDtypeStruct((M, D_FF), BF16),
        compiler_params=_cparams(("parallel", "arbitrary")),
        name="ffn_up",
    )(xb2, lw['w_ffn_in'], lw['w_ffn_in'])
    tm = _tile(M, 512)
    w_out = lw['w_ffn_out']
    return pl.pallas_call(
        _ffn_down_kernel,
        grid=(M // tm,),
        in_specs=[pl.BlockSpec((tm, D), lambda i: (i, 0)),
                  pl.BlockSpec((tm, D_FF), lambda i: (i, 0)),
                  pl.BlockSpec((None,) + w_out.shape[1:], lambda i: (l, 0, 0), pipeline_mode=pl.Buffered(1)),
                  _resident(lw['ln2_g'], l), _resident(lw['ln2_b'], l)],
        out_specs=[pl.BlockSpec((tm, D), lambda i: (i, 0))] * 2,
        out_shape=[jax.ShapeDtypeStruct((M, D), F32), jax.ShapeDtypeStruct((M, D), BF16)],
        compiler_params=_cparams(("parallel",), V7X_VMEM_LIMIT_MAX),
        name="ffn_down",
    )(x2, hid, w_out, lw['ln2_g'], lw['ln2_b'])


def _rot_half_rows(w):
    half = QK_ROPE // 2
    return jnp.concatenate([-w[..., half:, :], w[..., :half, :]], axis=-2)


def _rope_tables(pos):
    half = QK_ROPE // 2
    inv = ROPE_THETA ** (-jnp.arange(half, dtype=F32) / half)
    ang = pos.astype(F32)[:, None] * inv[None, :]
    cos = jnp.cos(ang)
    sin = jnp.sin(ang)
    cos64 = jnp.concatenate([cos, cos], -1)[None]
    sin64 = jnp.concatenate([sin, sin], -1)[None]
    return cos64, sin64, jnp.tile(cos64, (1, 1, MLA_HEADS)), jnp.tile(sin64, (1, 1, MLA_HEADS))


def _prep_weights(P):
    L = P['w_in'].shape[0]
    w_in_t = jnp.swapaxes(P['w_in'], 1, 2)
    D = w_in_t.shape[2]
    o_gm = Q_RANK + KV_RANK + QK_ROPE
    o_rw = o_gm + 2 * GM_WIDTH
    w_kr_t = w_in_t[:, Q_RANK + KV_RANK:o_gm]
    z64 = jnp.zeros((L, 64, D), F32)
    w_mla_t = jnp.concatenate([w_in_t[:, :Q_RANK + KV_RANK], w_kr_t, z64, _rot_half_rows(w_kr_t), z64], axis=1)
    w_uq_t = jnp.transpose(P['w_uq'], (0, 2, 3, 1))
    w_uq_rope_t = w_uq_t[:, :, QK_NOPE:]
    zda = jnp.zeros((L, D_DECAY, R_WIDTH), F32)
    wda = jnp.concatenate([jnp.concatenate([P['rw_w2'], zda], 2),
                           jnp.concatenate([zda, P['rw_a2']], 2)], 1)
    hid = jnp.arange(R_WIDTH) // R_HEAD_DIM
    row = lambda a: a.reshape(L, 1, -1)
    gm_w = P['gm_w_s']
    gm_b = P['gm_b_s']
    return {
        'w_mla_t': w_mla_t.astype(BF16),
        'q_norm_g': row(P['q_norm_g']), 'kv_norm_g': row(P['kv_norm_g']),
        'w_uq_nope_t': w_uq_t[:, :, :QK_NOPE].reshape(L, -1, Q_RANK).astype(BF16),
        'w_uq_rope_t': w_uq_rope_t.reshape(L, -1, Q_RANK).astype(BF16),
        'w_uq_rope_rot_t': _rot_half_rows(w_uq_rope_t).reshape(L, -1, Q_RANK).astype(BF16),
        'w_uk': P['w_uk'].astype(BF16),
        'w_uv': P['w_uv'].astype(BF16),
        'w_gm_t': w_in_t[:, o_gm:o_rw].astype(BF16),
        'gm_ln_g': row(P['gm_ln_g']), 'gm_ln_b': row(P['gm_ln_b']),
        'gm_w_s': gm_w,
        'gm_b_full': jnp.repeat(jnp.swapaxes(gm_b, 1, 2), GM_GROUP_W, axis=2),
        'gm_w00': row(jnp.repeat(gm_w[:, :, 0, 0], GM_GROUP_W, axis=1)),
        'gm_b0': row(jnp.repeat(gm_b[:, :, 0], GM_GROUP_W, axis=1)),
        'w_rw_t': w_in_t[:, o_rw:].astype(BF16),
        'rw_mu': row(P['rw_mu']),
        'rw_wda': wda.astype(BF16),
        'rw_w0a0': row(jnp.concatenate([P['rw_w0'], P['rw_a0']], axis=1)),
        'rw_g2': P['rw_g2'].astype(BF16),
        'rw_k_k': row(P['rw_k_k']), 'rw_k_a': row(P['rw_k_a']), 'rw_r_k': row(P['rw_r_k']),
        'rw_ln_g': row(P['rw_ln_g']), 'rw_ln_b': row(P['rw_ln_b']),
        'head_ind': (hid[:, None] == hid[None, :]).astype(BF16),
        'w_out': P['w_out'].astype(BF16),
        'ln1_g': row(P['ln1_g']), 'ln1_b': row(P['ln1_b']),
        'w_ffn_in': P['w_ffn_in'].astype(BF16),
        'w_ffn_out': P['w_ffn_out'].astype(BF16),
        'ln2_g': row(P['ln2_g']), 'ln2_b': row(P['ln2_b']),
    }


def _step_consts():
    n = R_HEAD_DIM
    sel = jnp.eye(R_WIDTH, dtype=BF16)
    rep = (jnp.arange(n * n)[:, None] // n == jnp.arange(n)[None, :]).astype(BF16)
    return sel, rep


def _layer_prompt(x, xb, tabs, lw):
    B, T, D = x.shape
    ckv, kr, kcat, qcat = _mla_in(xb, tabs, lw)
    y_mla = _attn_prompt(qcat, kcat, lw)
    y_gm, = _gmlp(xb.reshape(B * T, D), lw, single=False)
    shift0 = jnp.zeros((B, 1, R_SHIFT_W), F32)
    A, Bm, K, R, V, pc, g, bonus, shift = _rw_in(xb, shift0, lw, seq=True)
    y, S = _wkv_prompt(A, Bm, K, R, V, pc)
    y_rw = _rw_post(y.reshape(B * T, R_WIDTH), bonus.reshape(B * T, R_WIDTH), g.reshape(B * T, R_WIDTH), lw)
    x2, xb2 = _out_ln(x.reshape(B * T, D), y_mla.reshape(B * T, MLA_WIDTH), y_gm, y_rw, lw)
    x2, xb2 = _ffn(x2, xb2, lw)
    return x2.reshape(B, T, D), xb2.reshape(B, T, D), ckv, kr, S, shift.reshape(B, R_SHIFT_W)


def _layer_sample(x, xb, tabs, lw, consts, page_table, cache_c, cache_r, shift0, S_all):
    _, Bd, D = x.shape
    ckv, kr, _, qcat = _mla_in(xb, tabs, lw)
    tr = lambda a: jnp.swapaxes(a[0], 0, 1)
    y_mla = _attn_sample(page_table, tr(qcat), ckv.reshape(Bd, 1, KV_RANK),
                         kr.reshape(Bd, 1, QK_ROPE), lw, cache_c, cache_r)
    y_gm, v_gm = _gmlp(xb.reshape(Bd, D), lw, single=True)
    r, lgw, k, v, kk, b, g, bonus, shift = _rw_in(xb, shift0[None], lw, seq=False)
    y, S = _wkv_step(r, lgw, k, v, kk, b, S_all, lw['layer'], consts)
    y_rw = _rw_post(y, bonus[0], g[0], lw)
    x2, xb2 = _out_ln(x.reshape(Bd, D), y_mla.reshape(Bd, MLA_WIDTH), y_gm, y_rw, lw)
    x2, xb2 = _ffn(x2, xb2, lw)
    return (x2.reshape(1, Bd, D), xb2.reshape(1, Bd, D), ckv.reshape(Bd, 1, KV_RANK), kr.reshape(Bd, 1, QK_ROPE),
            S, shift[0], v_gm.reshape(Bd, 1, GM_WIDTH))


def kernel(x_prompt, x_sample, cache_c_kv, cache_k_rope, state_rwkv_wkv, state_rwkv_shift, page_table, w_in, q_norm_g, kv_norm_g, w_uq, w_uk, w_uv, gm_ln_g, gm_ln_b, gm_w_s, gm_b_s, rw_mu, rw_w0, rw_w2, rw_a0, rw_a2, rw_g2, rw_k_k, rw_k_a, rw_r_k, rw_ln_g, rw_ln_b, w_out, ln1_g, ln1_b, w_ffn_in, w_ffn_out, ln2_g, ln2_b):
    P = dict(w_in=w_in, q_norm_g=q_norm_g, kv_norm_g=kv_norm_g, w_uq=w_uq, w_uk=w_uk, w_uv=w_uv,
             gm_ln_g=gm_ln_g, gm_ln_b=gm_ln_b, gm_w_s=gm_w_s, gm_b_s=gm_b_s, rw_mu=rw_mu, rw_w0=rw_w0,
             rw_w2=rw_w2, rw_a0=rw_a0, rw_a2=rw_a2, rw_g2=rw_g2, rw_k_k=rw_k_k, rw_k_a=rw_k_a,
             rw_r_k=rw_r_k, rw_ln_g=rw_ln_g, rw_ln_b=rw_ln_b, w_out=w_out, ln1_g=ln1_g, ln1_b=ln1_b,
             w_ffn_in=w_ffn_in, w_ffn_out=w_ffn_out, ln2_g=ln2_g, ln2_b=ln2_b)
    n_dec, t_dec, _ = x_sample.shape
    assert t_dec == 1
    t_prompt = x_prompt.shape[1]
    past_len = page_table.shape[1] * PAGE_SIZE
    tabs_p = _rope_tables(jnp.arange(t_prompt, dtype=jnp.int32))
    tabs_s = _rope_tables(jnp.full((n_dec,), past_len, dtype=jnp.int32))
    consts = _step_consts()
    cache_k_rope_t = jnp.swapaxes(cache_k_rope, 2, 3)
    wkv_t = jnp.transpose(state_rwkv_wkv, (0, 2, 3, 4, 1))
    xp = x_prompt
    xs = x_sample.reshape(1, n_dec, -1)
    xpb, xsb = xp.astype(BF16), xs.astype(BF16)
    outs = [[] for _ in range(9)]
    W = _prep_weights(P)
    for l in range(w_in.shape[0]):
        lw = dict(W, layer=l)
        xp, xpb, ckv_p, kr_p, S_p, sh_p = _layer_prompt(xp, xpb, tabs_p, lw)
        xs, xsb, ckv_s, kr_s, S_s, sh_s, vg_s = _layer_sample(xs, xsb, tabs_s, lw, consts, page_table, cache_c_kv,
                                                              cache_k_rope_t, state_rwkv_shift[l], wkv_t)
        for lst, a in zip(outs, (ckv_p, kr_p, ckv_s, kr_s, S_p, sh_p, S_s, sh_s, vg_s)):
            lst.append(a)
    outs = [jnp.stack(o) for o in outs]
    outs[6] = jnp.transpose(outs[6], (0, 4, 1, 2, 3))
    return (xp, xs.reshape(n_dec, 1, -1)) + tuple(outs)
```

```python
import functools

import jax
import jax.numpy as jnp
from jax import lax
from jax.experimental import pallas as pl
from jax.experimental.pallas import tpu as pltpu

F32 = jnp.float32
BF16 = jnp.bfloat16

D_MODEL = 2048
DEPTH = 4
PAGE_SIZE = 128
MLA_HEADS = 8
QK_NOPE = 128
QK_ROPE = 64
V_HEAD = 128
Q_RANK = 512
KV_RANK = 256
ROPE_THETA = 10000.0
MLA_WIDTH = MLA_HEADS * V_HEAD
SM_SCALE = (QK_NOPE + QK_ROPE) ** -0.5
LOG2_E = 1.4426950408889634
Q_PRESCALE = SM_SCALE * LOG2_E
QK_CAT = KV_RANK + QK_ROPE
GM_GROUPS = 4
GM_CHUNK = 128
GM_GROUP_W = 128
GM_WIDTH = GM_GROUPS * GM_GROUP_W
R_HEADS = 8
R_HEAD_DIM = 64
R_WIDTH = R_HEADS * R_HEAD_DIM
D_DECAY = 64
D_AAA = 64
D_GATE = 128
R_SHIFT_W = 3 * R_WIDTH + D_DECAY + D_AAA + D_GATE
GN_EPS = 64e-5
D_FF = -(-8 * D_MODEL // (3 * 256)) * 256
ALPHA = (2 * DEPTH) ** 0.25
WKV_CHUNK = 64
V7X_VMEM_LIMIT = 56 * 1024 * 1024
V7X_VMEM_LIMIT_MAX = 60 * 1024 * 1024
LANES = 128


def _cparams(sem, vmem_limit=V7X_VMEM_LIMIT):
    return pltpu.CompilerParams(dimension_semantics=sem, vmem_limit_bytes=vmem_limit)


def _tile(n, pref):
    t = min(n, pref)
    while n % t:
        t //= 2
    return t


def _resident(a, layer=None):
    if layer is None:
        return pl.BlockSpec(a.shape, lambda *_: (0,) * a.ndim)
    return pl.BlockSpec((None,) + a.shape[1:], lambda *_: (layer,) + (0,) * (a.ndim - 1))


def _nt(a, b):
    return lax.dot_general(a, b, (((1,), (1,)), ((), ())), preferred_element_type=F32)


def _split3(x):
    hi = x.astype(BF16)
    r1 = x - hi.astype(F32)
    mid = r1.astype(BF16)
    lo = (r1 - mid.astype(F32)).astype(BF16)
    return hi, mid, lo


def _repeat(x, n, axis):
    return jnp.concatenate([x] * n, axis=axis)


def _tn(a, b):
    return lax.dot_general(a, b, (((0,), (0,)), ((), ())), preferred_element_type=F32)


def _exact_apply(f, x):
    hi, mid, lo = _split3(x)
    return (f(hi) + f(mid)) + f(lo)


def _sel_dot(x, e):
    return _exact_apply(lambda p: jnp.dot(p, e, preferred_element_type=F32), x)


def _head_sums(x, e):
    e2 = e[:LANES, :LANES]
    tiles = [_sel_dot(x[:, c:c + LANES], e2) for c in range(0, x.shape[1], LANES)]
    return jnp.concatenate(tiles, axis=1)


def _sel_dot_left(e, x):
    return _exact_apply(lambda p: jnp.dot(e, p, preferred_element_type=F32), x)


def _layer_norm_rows(x, g, b, eps):
    xc = x - jnp.mean(x, -1, keepdims=True)
    var = jnp.mean(xc * xc, -1, keepdims=True)
    return xc * lax.rsqrt(var + eps) * g + b


def _mla_in_kernel(x_ref, w_ref, qg_ref, kvg_ref, cos64_ref, sin64_ref, cos512_ref, sin512_ref,
                   wqn_ref, wqr_ref, wqrr_ref, wuk_ref,
                   ckv_ref, kr_ref, kcat_ref, qcat_ref):
    x = x_ref[0].astype(BF16)
    p = _nt(x, w_ref[...])
    cq = p[:, :Q_RANK]
    cq = cq * lax.rsqrt(jnp.mean(cq * cq, -1, keepdims=True) + 1e-6) * qg_ref[...]
    ckv = p[:, Q_RANK:Q_RANK + KV_RANK]
    ckv = ckv * lax.rsqrt(jnp.mean(ckv * ckv, -1, keepdims=True) + 1e-6) * kvg_ref[...]
    kr = p[:, 768:832] * cos64_ref[0] + p[:, 896:960] * sin64_ref[0]
    ckv_ref[0] = ckv
    kr_ref[0] = kr
    kcat_ref[0, :, :KV_RANK] = ckv.astype(BF16)
    kcat_ref[0, :, KV_RANK:] = kr.astype(BF16)
    cqb = cq.astype(BF16)
    qn = _nt(cqb, wqn_ref[...])
    qr = _nt(cqb, wqr_ref[...])
    qrr = _nt(cqb, wqrr_ref[...])
    qrope = (qr * cos512_ref[0] + qrr * sin512_ref[0]) * Q_PRESCALE
    for h in range(MLA_HEADS):
        ql = _nt(qn[:, h * QK_NOPE:(h + 1) * QK_NOPE].astype(BF16), wuk_ref[h])
        qcat_ref[0, h, :, :KV_RANK] = (ql * Q_PRESCALE).astype(BF16)
        qcat_ref[0, h, :, KV_RANK:] = qrope[:, h * QK_ROPE:(h + 1) * QK_ROPE].astype(BF16)


def _mla_in(x, tabs, lw):
    B, T, D = x.shape
    tm = _tile(T, 512)
    cos64, sin64, cos512, sin512 = tabs
    row = lambda w: pl.BlockSpec((1, tm, w), lambda b, t: (b, t, 0))
    tab = lambda w: pl.BlockSpec((1, tm, w), lambda b, t: (0, t, 0))
    full = lambda a: _resident(a, lw['layer'])
    hd = lambda w: pl.BlockSpec((1, MLA_HEADS, tm, w), lambda b, t: (b, 0, t, 0))
    ws = (lw['w_mla_t'], lw['q_norm_g'], lw['kv_norm_g'])
    ws2 = (lw['w_uq_nope_t'], lw['w_uq_rope_t'], lw['w_uq_rope_rot_t'], lw['w_uk'])
    return pl.pallas_call(
        _mla_in_kernel,
        grid=(B, T // tm),
        in_specs=[row(D)] + [full(a) for a in ws] + [tab(64), tab(64), tab(512), tab(512)] + [full(a) for a in ws2],
        out_specs=[row(KV_RANK), row(QK_ROPE), row(QK_CAT), hd(QK_CAT)],
        out_shape=[jax.ShapeDtypeStruct((B, T, KV_RANK), F32), jax.ShapeDtypeStruct((B, T, QK_ROPE), F32),
                   jax.ShapeDtypeStruct((B, T, QK_CAT), BF16),
                   jax.ShapeDtypeStruct((B, MLA_HEADS, T, QK_CAT), BF16)],
        compiler_params=_cparams(("parallel", "parallel")),
        name="mla_in",
    )(x, *ws, cos64, sin64, cos512, sin512, *ws2)


ATTN_HEAD_GROUP = 2


def _attn_kernel(qi_ref, kj_ref, q_ref, k_ref, wuv_ref, o_ref, m_sc, l_sc, acc_sc, *, tq, tk):
    step = pl.program_id(1)
    qi = qi_ref[step]
    kj = kj_ref[step]
    H = MLA_HEADS
    hg = ATTN_HEAD_GROUP
    n_groups = H // hg
    R = hg * tq

    @pl.when(kj == 0)
    def _():
        m_sc[...] = jnp.full(m_sc.shape, -jnp.inf, F32)
        l_sc[...] = jnp.zeros(l_sc.shape, F32)
        acc_sc[...] = jnp.zeros(acc_sc.shape, F32)

    def update(masked):
        k = k_ref[0]
        kc = k_ref[0, :, :KV_RANK]
        if masked:
            q_pos = qi * tq + lax.broadcasted_iota(jnp.int32, (tq, tk), 0)
            k_pos = kj * tk + lax.broadcasted_iota(jnp.int32, (tq, tk), 1)
            visible = (k_pos <= q_pos)[None]

        def scores(g):
            return _nt(q_ref[0, g * hg:(g + 1) * hg].reshape(R, QK_CAT), k)

        def softmax(g, s):
            rows = slice(g * R, (g + 1) * R)
            if masked:
                s = jnp.where(visible, s.reshape(hg, tq, tk), -jnp.inf).reshape(R, tk)
            m_old = m_sc[rows]
            m_new = jnp.maximum(m_old, jnp.max(s, -1, keepdims=True))
            scale = jnp.exp2(m_old - m_new)
            p = jnp.exp2(s - _repeat(m_new, tk // LANES, axis=1))
            l_sc[rows] = scale * l_sc[rows] + jnp.sum(p, -1, keepdims=True)
            m_sc[rows] = m_new
            return p.astype(BF16), scale

        def values(g, p, scale):
            rows = slice(g * R, (g + 1) * R)
            acc_sc[rows] = (_repeat(scale, KV_RANK // LANES, axis=1) * acc_sc[rows]
                            + jnp.dot(p, kc, preferred_element_type=F32))

        s_next = scores(0)
        pending = None
        for g in range(n_groups):
            s_cur = s_next
            if g + 1 < n_groups:
                s_next = scores(g + 1)
            if pending is not None:
                values(*pending)
            pending = (g,) + softmax(g, s_cur)
        values(*pending)

    needs_mask = kj * tk + tk - 1 > qi * tq
    pl.when(needs_mask)(functools.partial(update, True))
    pl.when(jnp.logical_not(needs_mask))(functools.partial(update, False))

    @pl.when(kj == (qi * tq + tq - 1) // tk)
    def _():
        o = (acc_sc[...] / _repeat(l_sc[...], KV_RANK // LANES, axis=1)).astype(BF16)
        for h in range(H):
            o_ref[0, :, h * V_HEAD:(h + 1) * V_HEAD] = jnp.dot(
                o[h * tq:(h + 1) * tq], wuv_ref[h], preferred_element_type=F32).astype(o_ref.dtype)


def _attn_prompt(qcat, kcat, lw):
    B, H, T, _ = qcat.shape
    w_uv = lw['w_uv']
    tq = _tile(T, 256)
    tk = _tile(T, 512)
    pairs = [(i, j) for i in range(T // tq) for j in range((i * tq + tq - 1) // tk + 1)]
    qi_of = jnp.asarray([p[0] for p in pairs], jnp.int32)
    kj_of = jnp.asarray([p[1] for p in pairs], jnp.int32)
    grid_spec = pltpu.PrefetchScalarGridSpec(
        num_scalar_prefetch=2,
        grid=(B, len(pairs)),
        in_specs=[pl.BlockSpec((1, H, tq, QK_CAT), lambda b, p, qi, kj: (b, 0, qi[p], 0)),
                  pl.BlockSpec((1, tk, QK_CAT), lambda b, p, qi, kj: (b, kj[p], 0)),
                  _resident(w_uv, lw['layer'])],
        out_specs=pl.BlockSpec((1, tq, MLA_WIDTH), lambda b, p, qi, kj: (b, qi[p], 0)),
        scratch_shapes=[pltpu.VMEM((H * tq, LANES), F32), pltpu.VMEM((H * tq, LANES), F32),
                        pltpu.VMEM((H * tq, KV_RANK), F32)])
    return pl.pallas_call(
        functools.partial(_attn_kernel, tq=tq, tk=tk),
        grid_spec=grid_spec,
        out_shape=jax.ShapeDtypeStruct((B, T, MLA_WIDTH), BF16),
        compiler_params=_cparams(("parallel", "arbitrary")),
        name="attn_prompt",
    )(qi_of, kj_of, qcat, kcat, w_uv)


ATTN_SLOTS = 3


def _attn_sample_kernel(pt_ref, q_ref, cnew_ref, rnew_ref, wuv_ref, cc_hbm, cr_hbm, o_ref,
                        cbuf, rbuf, sem, s_sc, kb_sc, *, layer, n_pages, chunk):
    b = pl.program_id(0)
    nb = pl.num_programs(0)
    H = MLA_HEADS
    n_keys = n_pages * PAGE_SIZE

    n_chunks = n_keys // chunk
    pages_per_chunk = n_pages // n_chunks

    def copies(bb, slot, pages=range(n_pages)):
        out = []
        for j in pages:
            pg = pt_ref[bb, j]
            keys = pl.ds(j * PAGE_SIZE, PAGE_SIZE)
            out.append(pltpu.make_async_copy(cc_hbm.at[layer, pg], cbuf.at[slot, keys], sem.at[0, slot]))
            out.append(pltpu.make_async_copy(cr_hbm.at[layer, pg], rbuf.at[slot, j], sem.at[1, slot]))
        return out

    @pl.when(b == 0)
    def _():
        for c in copies(0, 0) + copies(1, 1):
            c.start()

    slot = lax.rem(b, ATTN_SLOTS)
    slot_ahead = lax.rem(b + 2, ATTN_SLOTS)
    seq_ahead = jnp.minimum(b + 2, nb - 1)

    for c in copies(b, slot):
        c.wait()

    ql = q_ref[0, :, :KV_RANK]
    qr = q_ref[0, :, KV_RANK:]
    for c in range(n_chunks):
        keys = slice(c * chunk, (c + 1) * chunk)
        kc = cbuf[slot, keys].astype(BF16)
        kb_sc[keys] = kc
        pages = range(c * pages_per_chunk, (c + 1) * pages_per_chunk)
        kr_t = jnp.concatenate([rbuf[slot, j] for j in pages], axis=1).astype(BF16)
        s_sc[:, keys] = _nt(ql, kc) + jnp.dot(qr, kr_t, preferred_element_type=F32)
        for cp in copies(seq_ahead, slot_ahead, range(c * pages_per_chunk, (c + 1) * pages_per_chunk)):
            cp.start()
    cn = cnew_ref[0].astype(BF16)
    s_new = (jnp.sum(ql.astype(F32) * cn.astype(F32), -1, keepdims=True)
             + jnp.sum(qr.astype(F32) * rnew_ref[0].astype(BF16).astype(F32), -1, keepdims=True))
    s = s_sc[...]
    m = jnp.maximum(jnp.max(s, -1, keepdims=True), s_new)
    p = jnp.exp2(s - m)
    p_new = jnp.exp2(s_new - m)
    l = jnp.sum(p, -1, keepdims=True) + p_new
    p = (p / l).astype(BF16)
    parts = [(p_new / l).astype(BF16).astype(F32) * cn.astype(F32)]
    for c in range(n_chunks):
        keys = slice(c * chunk, (c + 1) * chunk)
        parts.append(jnp.dot(p[:, keys], kb_sc[keys], preferred_element_type=F32))
    while len(parts) > 1:
        parts = [a + b for a, b in zip(parts[::2], parts[1::2])] + parts[len(parts) - len(parts) % 2:]
    ob = parts[0].astype(BF16)
    for h in range(H):
        o_ref[0, :, h * V_HEAD:(h + 1) * V_HEAD] = jnp.dot(
            ob[h:h + 1], wuv_ref[h], preferred_element_type=F32).astype(o_ref.dtype)

    @pl.when(b == nb - 1)
    def _():
        for c in copies(b, lax.rem(b + 1, ATTN_SLOTS)) + copies(b, slot_ahead):
            c.wait()


def _attn_sample(page_table, qcat, ckv_new, kr_new, lw, cache_c, cache_rt):
    B, n_pages = page_table.shape
    H = MLA_HEADS
    n_keys = n_pages * PAGE_SIZE
    chunk = _tile(n_keys, 1024)
    layer, w_uv = lw['layer'], lw['w_uv']
    assert B >= 2 and n_keys % chunk == 0 and chunk % PAGE_SIZE == 0
    grid_spec = pltpu.PrefetchScalarGridSpec(
        num_scalar_prefetch=1,
        grid=(B,),
        in_specs=[pl.BlockSpec((1, H, QK_CAT), lambda b, pt: (b, 0, 0)),
                  pl.BlockSpec((1, 1, KV_RANK), lambda b, pt: (b, 0, 0)),
                  pl.BlockSpec((1, 1, QK_ROPE), lambda b, pt: (b, 0, 0)),
                  _resident(w_uv, layer),
                  pl.BlockSpec(memory_space=pl.ANY),
                  pl.BlockSpec(memory_space=pl.ANY)],
        out_specs=pl.BlockSpec((1, 1, MLA_WIDTH), lambda b, pt: (b, 0, 0)),
        scratch_shapes=[pltpu.VMEM((ATTN_SLOTS, n_keys, KV_RANK), F32),
                        pltpu.VMEM((ATTN_SLOTS, n_pages, QK_ROPE, PAGE_SIZE), F32),
                        pltpu.SemaphoreType.DMA((2, ATTN_SLOTS)),
                        pltpu.VMEM((H, n_keys), F32),
                        pltpu.VMEM((n_keys, KV_RANK), BF16)])
    return pl.pallas_call(
        functools.partial(_attn_sample_kernel, layer=layer, n_pages=n_pages, chunk=chunk),
        grid_spec=grid_spec,
        out_shape=jax.ShapeDtypeStruct((B, 1, MLA_WIDTH), BF16),
        compiler_params=_cparams(("arbitrary",)),
        name="attn_sample",
    )(page_table, qcat, ckv_new, kr_new, w_uv, cache_c, cache_rt)


def _gelu_tanh(x):
    return 0.5 * x * (1.0 + jnp.tanh(0.7978845608028654 * (x + 0.044715 * (x * x * x))))


def _gm_kernel(x_ref, w_ref, lng_ref, lnb_ref, ws_ref, bs_ref, y_ref, *v_out, tm, single):
    x = x_ref[...].astype(BF16)
    z = _gelu_tanh(_nt(x, w_ref[...]))
    u = z[:, :GM_WIDTH]
    v = _layer_norm_rows(z[:, GM_WIDTH:], lng_ref[...], lnb_ref[...], 1e-5)
    if single:
        y_ref[...] = (u * (v * ws_ref[...] + bs_ref[...])).astype(y_ref.dtype)
        v_out[0][...] = v
        return
    t_i = lax.broadcasted_iota(jnp.int32, (GM_CHUNK, GM_CHUNK), 0)
    s_i = lax.broadcasted_iota(jnp.int32, (GM_CHUNK, GM_CHUNK), 1)
    wm = [jnp.where(s_i <= t_i, ws_ref[g], 0.0).astype(BF16) for g in range(GM_GROUPS)]
    vb = v.astype(BF16)
    for c in range(tm // GM_CHUNK):
        rows = slice(c * GM_CHUNK, (c + 1) * GM_CHUNK)
        for g in range(GM_GROUPS):
            cols = slice(g * GM_GROUP_W, (g + 1) * GM_GROUP_W)
            mixed = jnp.dot(wm[g], vb[rows, cols], preferred_element_type=F32) + bs_ref[:, cols]
            y_ref[rows, cols] = (u[rows, cols] * mixed).astype(y_ref.dtype)


def _gmlp(x2, lw, single):
    M, D = x2.shape
    tm = _tile(M, 512)
    ws, bs = (lw['gm_w00'], lw['gm_b0']) if single else (lw['gm_w_s'], lw['gm_b_full'])
    full = lambda a: _resident(a, lw['layer'])
    row = lambda w: pl.BlockSpec((tm, w), lambda i: (i, 0))
    out_specs = [row(GM_WIDTH)]
    out_shape = [jax.ShapeDtypeStruct((M, GM_WIDTH), BF16)]
    if single:
        out_specs.append(row(GM_WIDTH))
        out_shape.append(jax.ShapeDtypeStruct((M, GM_WIDTH), F32))
    ins = (lw['w_gm_t'], lw['gm_ln_g'], lw['gm_ln_b'], ws, bs)
    return pl.pallas_call(
        functools.partial(_gm_kernel, tm=tm, single=single),
        grid=(M // tm,),
        in_specs=[row(D)] + [full(a) for a in ins],
        out_specs=out_specs, out_shape=out_shape,
        compiler_params=_cparams(("parallel",)),
        name="gmlp",
    )(x2, *ins)


def _rw_in_kernel(x_ref, prev_ref, w_ref, mu_ref, wda_ref, w0a0_ref, g2_ref, kk_ref_w, ka_ref, rk_ref, e_ref,
                  *refs, tm, seq):
    outs, (g_o, bonus_o, shift_o, carry) = refs[:6], refs[6:]
    t = pl.program_id(1)
    x = x_ref[0].astype(BF16)
    p = _nt(x, w_ref[...])
    if seq:
        @pl.when(t == 0)
        def _():
            carry[...] = prev_ref[0]
        row = lax.broadcasted_iota(jnp.int32, (tm, 1), 0)
        prev = jnp.where(row == 0, carry[...], pltpu.roll(p, 1, axis=0))
        carry[...] = p[tm - 1:tm]
    else:
        prev = prev_ref[0]
    shift_o[0] = p[tm - 1:tm] if seq else p
    xm = p + (prev - p) * mu_ref[...]
    r = xm[:, :R_WIDTH]
    k = xm[:, R_WIDTH:2 * R_WIDTH]
    v = xm[:, 2 * R_WIDTH:3 * R_WIDTH]
    da = xm[:, 3 * R_WIDTH:3 * R_WIDTH + D_DECAY + D_AAA]
    lane = lax.broadcasted_iota(jnp.int32, da.shape, 1)
    da = jnp.where(lane < D_DECAY, jnp.tanh(da), da)
    m = jnp.dot(da.astype(BF16), wda_ref[...], preferred_element_type=F32) + w0a0_ref[...]
    wv = m[:, :R_WIDTH]
    w_log = -(jnp.maximum(-wv, 0.0) + jnp.log1p(jnp.exp(-jnp.abs(wv)))) - 0.5
    logdecay = -jnp.exp(w_log)
    a = jax.nn.sigmoid(m[:, R_WIDTH:])
    gd = xm[:, 3 * R_WIDTH + D_DECAY + D_AAA:]
    g = jnp.dot(jax.nn.sigmoid(gd).astype(BF16), g2_ref[...], preferred_element_type=F32)
    e = e_ref[...]
    kk = k * kk_ref_w[...]
    kk = kk * lax.rsqrt(jnp.maximum(_head_sums(kk * kk, e), 1e-24))
    k2 = k * (1.0 + (a - 1.0) * ka_ref[...])
    bonus = _head_sums(r * k2 * rk_ref[...], e) * v
    g_o[0] = g
    bonus_o[0] = bonus
    bb = kk * a
    if seq:
        C = WKV_CHUNK
        ti = lax.broadcasted_iota(jnp.int32, (C, C), 0)
        si = lax.broadcasted_iota(jnp.int32, (C, C), 1)
        tri = jnp.where(si <= ti, 1.0, 0.0).astype(BF16)
        ci = lax.broadcasted_iota(jnp.int32, (tm // C, tm), 0)
        sj = lax.broadcasted_iota(jnp.int32, (tm // C, tm), 1)
        in_chunk = jnp.where(sj // C == ci, 1.0, 0.0).astype(BF16)
        cum = jnp.concatenate([_sel_dot_left(tri, logdecay[c * C:(c + 1) * C]) for c in range(tm // C)], axis=0)
        p_in = jnp.exp(cum)
        p_inv = jnp.exp(-cum)
        pc_o = outs[5]
        pc_o[0] = jnp.exp(_sel_dot_left(in_chunk, logdecay))
        vals = (-kk * jnp.exp(cum - logdecay), bb * p_inv, k2 * p_inv, r * p_in, v)
        for h in range(R_HEADS):
            c = slice(h * R_HEAD_DIM, (h + 1) * R_HEAD_DIM)
            for o_ref, val in zip(outs, vals):
                o_ref[0, h] = val[:, c].astype(o_ref.dtype)
    else:
        for o_ref, val in zip(outs, (r, logdecay, k2, v, kk, bb)):
            o_ref[0] = val


def _rw_in(x, prev, lw, seq):
    B, T, D = x.shape
    tm = _tile(T, 512 if seq else 256)
    row = lambda w: pl.BlockSpec((1, tm, w), lambda b, t: (b, t, 0))
    hd = pl.BlockSpec((1, R_HEADS, tm, R_HEAD_DIM), lambda b, t: (b, 0, t, 0))
    ws = (lw['w_rw_t'], lw['rw_mu'], lw['rw_wda'], lw['rw_w0a0'], lw['rw_g2'], lw['rw_k_k'], lw['rw_k_a'],
          lw['rw_r_k'])
    w_specs = [_resident(a, lw['layer']) for a in ws] + [_resident(lw['head_ind'])]
    ws = ws + (lw['head_ind'],)
    hshape = lambda dt: jax.ShapeDtypeStruct((B, R_HEADS, T, R_HEAD_DIM), dt)
    if seq:
        C = WKV_CHUNK
        prev_spec = pl.BlockSpec((1, 1, R_SHIFT_W), lambda b, t: (b, 0, 0))
        shift_spec = pl.BlockSpec((1, 1, R_SHIFT_W), lambda b, t: (b, 0, 0))
        shift_shape = jax.ShapeDtypeStruct((B, 1, R_SHIFT_W), F32)
        head_specs = [hd] * 5 + [pl.BlockSpec((1, tm // C, R_WIDTH), lambda b, t: (b, t, 0))]
        head_shapes = [hshape(BF16)] * 5 + [jax.ShapeDtypeStruct((B, T // C, R_WIDTH), F32)]
    else:
        prev_spec = row(R_SHIFT_W)
        shift_spec = row(R_SHIFT_W)
        shift_shape = jax.ShapeDtypeStruct((B, T, R_SHIFT_W), F32)
        head_specs = [row(R_WIDTH)] * 6
        head_shapes = [jax.ShapeDtypeStruct((B, T, R_WIDTH), F32)] * 6
    fshape = jax.ShapeDtypeStruct((B, T, R_WIDTH), F32)
    return pl.pallas_call(
        functools.partial(_rw_in_kernel, tm=tm, seq=seq),
        grid=(B, T // tm),
        in_specs=[row(D), prev_spec] + w_specs,
        out_specs=head_specs + [row(R_WIDTH), row(R_WIDTH), shift_spec],
        out_shape=head_shapes + [fshape, fshape, shift_shape],
        scratch_shapes=[pltpu.VMEM((1, R_SHIFT_W), F32)],
        compiler_params=_cparams(("parallel", "arbitrary")),
        name="rw_in",
    )(x, prev, *ws)


def _bdot(a, b):
    return jnp.dot(a.astype(BF16), b.astype(BF16), preferred_element_type=F32)


def _bdot_nt(a, b):
    return _nt(a.astype(BF16), b.astype(BF16))


def _bdot_tn(a, b):
    return _tn(a.astype(BF16), b.astype(BF16))


def _wkv_kernel(a_ref, b_ref, k_ref, r_ref, v_ref, pc_ref, y_ref, s_ref, state):
    c = pl.program_id(0)
    n_seq, n_head, C, _ = a_ref.shape

    @pl.when(c == 0)
    def _():
        state[...] = jnp.zeros(state.shape, F32)

    ti = lax.broadcasted_iota(jnp.int32, (C, 2 * C), 0)
    si = lax.broadcasted_iota(jnp.int32, (C, 2 * C), 1) % C
    incl2 = si <= ti
    strict2 = si < ti
    n_steps = C.bit_length() - 1
    ch = [(b, h) for b in range(n_seq) for h in range(n_head)]
    each = lambda f, *lists: [f(*args) for args in zip(*lists)]
    A = [a_ref[b, h] for b, h in ch]
    Bm = [b_ref[b, h] for b, h in ch]
    K = [k_ref[b, h] for b, h in ch]
    R = [r_ref[b, h] for b, h in ch]
    V = [v_ref[b, h] for b, h in ch]
    D = R_HEAD_DIM
    BK = each(lambda bm, k: jnp.concatenate([bm, k], axis=0), Bm, K)
    NM = each(lambda a, bk: jnp.where(strict2, _nt(a, bk), 0.0), A, BK)
    Nb = [nm[:, :C].astype(BF16) for nm in NM]
    XU = each(lambda nm, v: _bdot(nm[:, C:], v), NM, V)
    X = each(lambda a, xu: jnp.concatenate([a.astype(F32), xu], axis=1), A, XU)
    for i in range(n_steps):
        X = each(lambda n, x: x + _bdot(n, x), Nb, X)
        if i + 1 < n_steps:
            Nb = each(lambda n: _bdot(n, n).astype(BF16), Nb)
    RBK = each(lambda r, bk: jnp.where(incl2, _nt(r, bk), 0.0).astype(BF16), R, BK)
    RBX = each(lambda rbk, x: _bdot(rbk[:, :C], x), RBK, X)
    Q = each(lambda r, rbx: r.astype(F32) + rbx[:, :D], R, RBX)
    Y0 = each(lambda rbx, rbk, v: rbx[:, D:] + _bdot(rbk[:, C:], v), RBX, RBK, V)
    GH = each(_bdot_tn, X, Bm)
    G = [gh[:D] for gh in GH]
    Hm = each(lambda gh, v, k: gh[D:] + _bdot_tn(v, k), GH, V, K)
    S0 = [state[b, h] for b, h in ch]
    Y = each(lambda q, s0, y0: _bdot_nt(q, s0) + y0, Q, S0, Y0)
    pc = [pc_ref[b, 0, :, h * R_HEAD_DIM:(h + 1) * R_HEAD_DIM] for b, h in ch]
    S1 = each(lambda s0, g, hm, p: (s0 + _bdot(s0, g) + hm) * p, S0, G, Hm, pc)
    for (b, h), s1 in zip(ch, S1):
        state[b, h] = s1
        s_ref[b, h] = s1
    for b in range(n_seq):
        y_ref[b] = jnp.concatenate(Y[b * n_head:(b + 1) * n_head], axis=-1)


def _wkv_prompt(A, Bm, K, R, V, pc):
    B, H, T, N = A.shape
    C = WKV_CHUNK
    hd = pl.BlockSpec((B, H, C, N), lambda c: (0, 0, c, 0))
    return pl.pallas_call(
        _wkv_kernel,
        grid=(T // C,),
        in_specs=[hd] * 5 + [pl.BlockSpec((B, 1, 1, H * N), lambda c: (0, c, 0, 0))],
        out_specs=[pl.BlockSpec((B, C, H * N), lambda c: (0, c, 0)),
                   pl.BlockSpec((B, H, N, N), lambda c: (0, 0, 0, 0))],
        out_shape=[jax.ShapeDtypeStruct((B, T, H * N), F32), jax.ShapeDtypeStruct((B, H, N, N), F32)],
        scratch_shapes=[pltpu.VMEM((B, H, N, N), F32)],
        compiler_params=_cparams(("arbitrary",)),
        name="wkv_prompt",
    )(A, Bm, K, R, V, pc.reshape(B, T // C, 1, H * N))


def _wkv_step_kernel(r_ref, lw_ref, k_ref, v_ref, kk_ref, b_ref, s_ref, sel_ref, rep_ref, y_ref, so_ref):
    n = R_HEAD_DIM
    h = pl.program_id(0)
    sel = sel_ref[...]
    rep = rep_ref[...]
    head_t = lambda ref: _exact_apply(lambda p: _nt(sel, p), ref[0])
    over_i = lambda x: _repeat(x, n, axis=0)
    over_j = lambda x: _sel_dot_left(rep, x)
    sum_j = lambda x: _exact_apply(lambda p: _tn(rep, p), x)
    S = s_ref[...].reshape(n * n, s_ref.shape[-1])
    sa = sum_j(S * over_i(-head_t(kk_ref)))
    S1 = (S * over_i(jnp.exp(head_t(lw_ref))) + over_j(sa) * over_i(head_t(b_ref))
          + over_j(head_t(v_ref)) * over_i(head_t(k_ref)))
    so_ref[...] = S1.reshape(so_ref.shape)
    y_t = sum_j(S1 * over_i(head_t(r_ref)))

    @pl.when(h == 0)
    def _():
        y_ref[...] = jnp.zeros(y_ref.shape, F32)

    y_ref[...] += _exact_apply(lambda p: _tn(p, sel), y_t)


def _wkv_step(r, lw, k, v, kk, b, S_all, layer, consts):
    _, Bd, W = r.shape
    _, H, n, _, _ = S_all.shape
    sel, rep = consts
    vec = pl.BlockSpec((1, Bd, W), lambda h: (0, 0, 0))
    return pl.pallas_call(
        _wkv_step_kernel,
        grid=(H,),
        in_specs=[vec] * 6 + [pl.BlockSpec((None, None, n, n, Bd), lambda h: (layer, h, 0, 0, 0)),
                              pl.BlockSpec((n, W), lambda h: (h, 0)), _resident(rep)],
        out_specs=[pl.BlockSpec((Bd, W), lambda h: (0, 0)),
                   pl.BlockSpec((None, n, n, Bd), lambda h: (h, 0, 0, 0))],
        out_shape=[jax.ShapeDtypeStruct((Bd, W), F32), jax.ShapeDtypeStruct((H, n, n, Bd), F32)],
        compiler_params=_cparams(("arbitrary",)),
        name="wkv_step",
    )(r, lw, k, v, kk, b, S_all, sel, rep)


def _rw_post_kernel(y_ref, bonus_ref, g_ref, lng_ref, lnb_ref, e_ref, o_ref):
    e = e_ref[...]
    y = y_ref[...]
    mean = _head_sums(y, e) * (1.0 / R_HEAD_DIM)
    yc = y - mean
    var = _head_sums(yc * yc, e) * (1.0 / R_HEAD_DIM)
    yn = yc * lax.rsqrt(var + GN_EPS) * lng_ref[...] + lnb_ref[...]
    o_ref[...] = ((yn + bonus_ref[...]) * g_ref[...]).astype(o_ref.dtype)


def _rw_post(y, bonus, g, lw):
    M, W = y.shape
    tm = _tile(M, 512)
    row = pl.BlockSpec((tm, W), lambda i: (i, 0))
    ws = (lw['rw_ln_g'], lw['rw_ln_b'], lw['head_ind'])
    w_specs = [_resident(lw['rw_ln_g'], lw['layer']), _resident(lw['rw_ln_b'], lw['layer']),
               _resident(lw['head_ind'])]
    return pl.pallas_call(
        _rw_post_kernel,
        grid=(M // tm,),
        in_specs=[row, row, row] + w_specs,
        out_specs=row,
        out_shape=jax.ShapeDtypeStruct((M, W), BF16),
        compiler_params=_cparams(("parallel",)),
        name="rw_post",
    )(y, bonus, g, *ws)


def _out_ln_kernel(x_ref, ya_ref, yg_ref, yr_ref, w_ref, g_ref, b_ref, o_ref, ob_ref):
    h = jnp.dot(ya_ref[...], w_ref[:MLA_WIDTH], preferred_element_type=F32)
    h = h + jnp.dot(yg_ref[...], w_ref[MLA_WIDTH:MLA_WIDTH + GM_WIDTH], preferred_element_type=F32)
    h = h + jnp.dot(yr_ref[...], w_ref[MLA_WIDTH + GM_WIDTH:], preferred_element_type=F32)
    y = _layer_norm_rows(ALPHA * x_ref[...] + h, g_ref[...], b_ref[...], 1e-5)
    o_ref[...] = y
    ob_ref[...] = y.astype(BF16)


def _out_ln(x2, y_mla, y_gm, y_rw, lw):
    M, D = x2.shape
    tm = _tile(M, 512)
    row = lambda w: pl.BlockSpec((tm, w), lambda i: (i, 0))
    full = lambda a: _resident(a, lw['layer'])
    ws = (lw['w_out'], lw['ln1_g'], lw['ln1_b'])
    return pl.pallas_call(
        _out_ln_kernel,
        grid=(M // tm,),
        in_specs=[row(D), row(MLA_WIDTH), row(GM_WIDTH), row(R_WIDTH)] + [full(a) for a in ws],
        out_specs=[row(D), row(D)],
        out_shape=[jax.ShapeDtypeStruct((M, D), F32), jax.ShapeDtypeStruct((M, D), BF16)],
        compiler_params=_cparams(("parallel",)),
        name="out_ln",
    )(x2, y_mla, y_gm, y_rw, *ws)


def _ffn_up_kernel(xb_ref, wg_ref, wu_ref, h_ref):
    xb = xb_ref[...]
    gate = jnp.dot(xb, wg_ref[...], preferred_element_type=F32)
    up = jnp.dot(xb, wu_ref[...], preferred_element_type=F32)
    h_ref[...] = (gate * jax.nn.sigmoid(gate) * up).astype(h_ref.dtype)


def _ffn_down_kernel(x_ref, h_ref, wo_ref, g_ref, b_ref, o_ref, ob_ref):
    y = jnp.dot(h_ref[...], wo_ref[...], preferred_element_type=F32)
    y = _layer_norm_rows(ALPHA * x_ref[...] + y, g_ref[...], b_ref[...], 1e-5)
    o_ref[...] = y
    ob_ref[...] = y.astype(BF16)


def _ffn(x2, xb2, lw):
    M, D = x2.shape
    l = lw['layer']
    tm = _tile(M, 1024)
    tf = 512
    nf = D_FF // tf
    hid = pl.pallas_call(
        _ffn_up_kernel,
        grid=(M // tm, nf),
        in_specs=[pl.BlockSpec((tm, D), lambda i, j: (i, 0)),
                  pl.BlockSpec((None, D, tf), lambda i, j: (l, 0, j)),
                  pl.BlockSpec((None, D, tf), lambda i, j: (l, 0, j + nf))],
        out_specs=pl.BlockSpec((tm, tf), lambda i, j: (i, j)),
        out_shape=jax.ShapeDtypeStruct((M, D_FF), BF16),
        compiler_params=_cparams(("parallel", "arbitrary")),
        name="ffn_up",
    )(xb2, lw['w_ffn_in'], lw['w_ffn_in'])
    tm = _tile(M, 512)
    w_out = lw['w_ffn_out']
    return pl.pallas_call(
        _ffn_down_kernel,
        grid=(M // tm,),
        in_specs=[pl.BlockSpec((tm, D), lambda i: (i, 0)),
                  pl.BlockSpec((tm, D_FF), lambda i: (i, 0)),
                  pl.BlockSpec((None,) + w_out.shape[1:], lambda i: (l, 0, 0), pipeline_mode=pl.Buffered(1)),
                  _resident(lw['ln2_g'], l), _resident(lw['ln2_b'], l)],
        out_specs=[pl.BlockSpec((tm, D), lambda i: (i, 0))] * 2,
        out_shape=[jax.ShapeDtypeStruct((M, D), F32), jax.ShapeDtypeStruct((M, D), BF16)],
        compiler_params=_cparams(("parallel",), V7X_VMEM_LIMIT_MAX),
        name="ffn_down",
    )(x2, hid, w_out, lw['ln2_g'], lw['ln2_b'])


def _rot_half_rows(w):
    half = QK_ROPE // 2
    return jnp.concatenate([-w[..., half:, :], w[..., :half, :]], axis=-2)


def _rope_tables(pos):
    half = QK_ROPE // 2
    inv = ROPE_THETA ** (-jnp.arange(half, dtype=F32) / half)
    ang = pos.astype(F32)[:, None] * inv[None, :]
    cos = jnp.cos(ang)
    sin = jnp.sin(ang)
    cos64 = jnp.concatenate([cos, cos], -1)[None]
    sin64 = jnp.concatenate([sin, sin], -1)[None]
    return cos64, sin64, jnp.tile(cos64, (1, 1, MLA_HEADS)), jnp.tile(sin64, (1, 1, MLA_HEADS))


def _prep_weights(P):
    L = P['w_in'].shape[0]
    w_in_t = jnp.swapaxes(P['w_in'], 1, 2)
    D = w_in_t.shape[2]
    o_gm = Q_RANK + KV_RANK + QK_ROPE
    o_rw = o_gm + 2 * GM_WIDTH
    w_kr_t = w_in_t[:, Q_RANK + KV_RANK:o_gm]
    z64 = jnp.zeros((L, 64, D), F32)
    w_mla_t = jnp.concatenate([w_in_t[:, :Q_RANK + KV_RANK], w_kr_t, z64, _rot_half_rows(w_kr_t), z64], axis=1)
    w_uq_t = jnp.transpose(P['w_uq'], (0, 2, 3, 1))
    w_uq_rope_t = w_uq_t[:, :, QK_NOPE:]
    zda = jnp.zeros((L, D_DECAY, R_WIDTH), F32)
    wda = jnp.concatenate([jnp.concatenate([P['rw_w2'], zda], 2),
                           jnp.concatenate([zda, P['rw_a2']], 2)], 1)
    hid = jnp.arange(R_WIDTH) // R_HEAD_DIM
    row = lambda a: a.reshape(L, 1, -1)
    gm_w = P['gm_w_s']
    gm_b = P['gm_b_s']
    return {
        'w_mla_t': w_mla_t.astype(BF16),
        'q_norm_g': row(P['q_norm_g']), 'kv_norm_g': row(P['kv_norm_g']),
        'w_uq_nope_t': w_uq_t[:, :, :QK_NOPE].reshape(L, -1, Q_RANK).astype(BF16),
        'w_uq_rope_t': w_uq_rope_t.reshape(L, -1, Q_RANK).astype(BF16),
        'w_uq_rope_rot_t': _rot_half_rows(w_uq_rope_t).reshape(L, -1, Q_RANK).astype(BF16),
        'w_uk': P['w_uk'].astype(BF16),
        'w_uv': P['w_uv'].astype(BF16),
        'w_gm_t': w_in_t[:, o_gm:o_rw].astype(BF16),
        'gm_ln_g': row(P['gm_ln_g']), 'gm_ln_b': row(P['gm_ln_b']),
        'gm_w_s': gm_w,
        'gm_b_full': jnp.repeat(jnp.swapaxes(gm_b, 1, 2), GM_GROUP_W, axis=2),
        'gm_w00': row(jnp.repeat(gm_w[:, :, 0, 0], GM_GROUP_W, axis=1)),
        'gm_b0': row(jnp.repeat(gm_b[:, :, 0], GM_GROUP_W, axis=1)),
        'w_rw_t': w_in_t[:, o_rw:].astype(BF16),
        'rw_mu': row(P['rw_mu']),
        'rw_wda': wda.astype(BF16),
        'rw_w0a0': row(jnp.concatenate([P['rw_w0'], P['rw_a0']], axis=1)),
        'rw_g2': P['rw_g2'].astype(BF16),
        'rw_k_k': row(P['rw_k_k']), 'rw_k_a': row(P['rw_k_a']), 'rw_r_k': row(P['rw_r_k']),
        'rw_ln_g': row(P['rw_ln_g']), 'rw_ln_b': row(P['rw_ln_b']),
        'head_ind': (hid[:, None] == hid[None, :]).astype(BF16),
        'w_out': P['w_out'].astype(BF16),
        'ln1_g': row(P['ln1_g']), 'ln1_b': row(P['ln1_b']),
        'w_ffn_in': P['w_ffn_in'].astype(BF16),
        'w_ffn_out': P['w_ffn_out'].astype(BF16),
        'ln2_g': row(P['ln2_g']), 'ln2_b': row(P['ln2_b']),
    }


def _step_consts():
    n = R_HEAD_DIM
    sel = jnp.eye(R_WIDTH, dtype=BF16)
    rep = (jnp.arange(n * n)[:, None] // n == jnp.arange(n)[None, :]).astype(BF16)
    return sel, rep


def _layer_prompt(x, xb, tabs, lw):
    B, T, D = x.shape
    ckv, kr, kcat, qcat = _mla_in(xb, tabs, lw)
    y_mla = _attn_prompt(qcat, kcat, lw)
    y_gm, = _gmlp(xb.reshape(B * T, D), lw, single=False)
    shift0 = jnp.zeros((B, 1, R_SHIFT_W), F32)
    A, Bm, K, R, V, pc, g, bonus, shift = _rw_in(xb, shift0, lw, seq=True)
    y, S = _wkv_prompt(A, Bm, K, R, V, pc)
    y_rw = _rw_post(y.reshape(B * T, R_WIDTH), bonus.reshape(B * T, R_WIDTH), g.reshape(B * T, R_WIDTH), lw)
    x2, xb2 = _out_ln(x.reshape(B * T, D), y_mla.reshape(B * T, MLA_WIDTH), y_gm, y_rw, lw)
    x2, xb2 = _ffn(x2, xb2, lw)
    return x2.reshape(B, T, D), xb2.reshape(B, T, D), ckv, kr, S, shift.reshape(B, R_SHIFT_W)


def _layer_sample(x, xb, tabs, lw, consts, page_table, cache_c, cache_r, shift0, S_all):
    _, Bd, D = x.shape
    ckv, kr, _, qcat = _mla_in(xb, tabs, lw)
    tr = lambda a: jnp.swapaxes(a[0], 0, 1)
    y_mla = _attn_sample(page_table, tr(qcat), ckv.reshape(Bd, 1, KV_RANK),
                         kr.reshape(Bd, 1, QK_ROPE), lw, cache_c, cache_r)
    y_gm, v_gm = _gmlp(xb.reshape(Bd, D), lw, single=True)
    r, lgw, k, v, kk, b, g, bonus, shift = _rw_in(xb, shift0[None], lw, seq=False)
    y, S = _wkv_step(r, lgw, k, v, kk, b, S_all, lw['layer'], consts)
    y_rw = _rw_post(y, bonus[0], g[0], lw)
    x2, xb2 = _out_ln(x.reshape(Bd, D), y_mla.reshape(Bd, MLA_WIDTH), y_gm, y_rw, lw)
    x2, xb2 = _ffn(x2, xb2, lw)
    return (x2.reshape(1, Bd, D), xb2.reshape(1, Bd, D), ckv.reshape(Bd, 1, KV_RANK), kr.reshape(Bd, 1, QK_ROPE),
            S, shift[0], v_gm.reshape(Bd, 1, GM_WIDTH))


def kernel(x_prompt, x_sample, cache_c_kv, cache_k_rope, state_rwkv_wkv, state_rwkv_shift, page_table, w_in, q_norm_g, kv_norm_g, w_uq, w_uk, w_uv, gm_ln_g, gm_ln_b, gm_w_s, gm_b_s, rw_mu, rw_w0, rw_w2, rw_a0, rw_a2, rw_g2, rw_k_k, rw_k_a, rw_r_k, rw_ln_g, rw_ln_b, w_out, ln1_g, ln1_b, w_ffn_in, w_ffn_out, ln2_g, ln2_b):
    P = dict(w_in=w_in, q_norm_g=q_norm_g, kv_norm_g=kv_norm_g, w_uq=w_uq, w_uk=w_uk, w_uv=w_uv,
             gm_ln_g=gm_ln_g, gm_ln_b=gm_ln_b, gm_w_s=gm_w_s, gm_b_s=gm_b_s, rw_mu=rw_mu, rw_w0=rw_w0,
             rw_w2=rw_w2, rw_a0=rw_a0, rw_a2=rw_a2, rw_g2=rw_g2, rw_k_k=rw_k_k, rw_k_a=rw_k_a,
             rw_r_k=rw_r_k, rw_ln_g=rw_ln_g, rw_ln_b=rw_ln_b, w_out=w_out, ln1_g=ln1_g, ln1_b=ln1_b,
             w_ffn_in=w_ffn_in, w_ffn_out=w_ffn_out, ln2_g=ln2_g, ln2_b=ln2_b)
    n_dec, t_dec, _ = x_sample.shape
    assert t_dec == 1
    t_prompt = x_prompt.shape[1]
    past_len = page_table.shape[1] * PAGE_SIZE
    tabs_p = _rope_tables(jnp.arange(t_prompt, dtype=jnp.int32))
    tabs_s = _rope_tables(jnp.full((n_dec,), past_len, dtype=jnp.int32))
    consts = _step_consts()
    cache_k_rope_t = jnp.swapaxes(cache_k_rope, 2, 3)
    wkv_t = jnp.transpose(state_rwkv_wkv, (0, 2, 3, 4, 1))
    xp = x_prompt
    xs = x_sample.reshape(1, n_dec, -1)
    xpb, xsb = xp.astype(BF16), xs.astype(BF16)
    outs = [[] for _ in range(9)]
    W = _prep_weights(P)
    for l in range(w_in.shape[0]):
        lw = dict(W, layer=l)
        xp, xpb, ckv_p, kr_p, S_p, sh_p = _layer_prompt(xp, xpb, tabs_p, lw)
        xs, xsb, ckv_s, kr_s, S_s, sh_s, vg_s = _layer_sample(xs, xsb, tabs_s, lw, consts, page_table, cache_c_kv,
                                                              cache_k_rope_t, state_rwkv_shift[l], wkv_t)
        for lst, a in zip(outs, (ckv_p, kr_p, ckv_s, kr_s, S_p, sh_p, S_s, sh_s, vg_s)):
            lst.append(a)
    outs = [jnp.stack(o) for o in outs]
    outs[6] = jnp.transpose(outs[6], (0, 4, 1, 2, 3))
    return (xp, xs.reshape(n_dec, 1, -1)) + tuple(outs)
```

```python
import functools

import jax
import jax.numpy as jnp
from jax import lax
from jax.experimental import pallas as pl
from jax.experimental.pallas import tpu as pltpu

F32 = jnp.float32
BF16 = jnp.bfloat16

D_MODEL = 2048
DEPTH = 4
PAGE_SIZE = 128
MLA_HEADS = 8
QK_NOPE = 128
QK_ROPE = 64
V_HEAD = 128
Q_RANK = 512
KV_RANK = 256
ROPE_THETA = 10000.0
MLA_WIDTH = MLA_HEADS * V_HEAD
SM_SCALE = (QK_NOPE + QK_ROPE) ** -0.5
LOG2_E = 1.4426950408889634
Q_PRESCALE = SM_SCALE * LOG2_E
QK_CAT = KV_RANK + QK_ROPE
GM_GROUPS = 4
GM_CHUNK = 128
GM_GROUP_W = 128
GM_WIDTH = GM_GROUPS * GM_GROUP_W
R_HEADS = 8
R_HEAD_DIM = 64
R_WIDTH = R_HEADS * R_HEAD_DIM
D_DECAY = 64
D_AAA = 64
D_GATE = 128
R_SHIFT_W = 3 * R_WIDTH + D_DECAY + D_AAA + D_GATE
GN_EPS = 64e-5
D_FF = -(-8 * D_MODEL // (3 * 256)) * 256
ALPHA = (2 * DEPTH) ** 0.25
WKV_CHUNK = 64
WKV_CHUNKS_PER_STEP = 2
V7X_VMEM_LIMIT = 56 * 1024 * 1024
V7X_VMEM_LIMIT_MAX = 60 * 1024 * 1024
LANES = 128


def _cparams(sem, vmem_limit=V7X_VMEM_LIMIT):
    return pltpu.CompilerParams(dimension_semantics=sem, vmem_limit_bytes=vmem_limit)


def _tile(n, pref):
    t = min(n, pref)
    while n % t:
        t //= 2
    return t


def _resident(a, layer=None):
    if layer is None:
        return pl.BlockSpec(a.shape, lambda *_: (0,) * a.ndim)
    return pl.BlockSpec((None,) + a.shape[1:], lambda *_: (layer,) + (0,) * (a.ndim - 1))


def _nt(a, b):
    return lax.dot_general(a, b, (((1,), (1,)), ((), ())), preferred_element_type=F32)


def _split3(x):
    hi = x.astype(BF16)
    r1 = x - hi.astype(F32)
    mid = r1.astype(BF16)
    lo = (r1 - mid.astype(F32)).astype(BF16)
    return hi, mid, lo


def _repeat(x, n, axis):
    return jnp.concatenate([x] * n, axis=axis)


def _tn(a, b):
    return lax.dot_general(a, b, (((0,), (0,)), ((), ())), preferred_element_type=F32)


def _exact_apply(f, x):
    hi, mid, lo = _split3(x)
    return (f(hi) + f(mid)) + f(lo)


def _sel_dot(x, e):
    return _exact_apply(lambda p: jnp.dot(p, e, preferred_element_type=F32), x)


def _head_sums(x, e):
    e2 = e[:LANES, :LANES]
    tiles = [_sel_dot(x[:, c:c + LANES], e2) for c in range(0, x.shape[1], LANES)]
    return jnp.concatenate(tiles, axis=1)


def _sel_dot_left(e, x):
    return _exact_apply(lambda p: jnp.dot(e, p, preferred_element_type=F32), x)


def _layer_norm_rows(x, g, b, eps):
    xc = x - jnp.mean(x, -1, keepdims=True)
    var = jnp.mean(xc * xc, -1, keepdims=True)
    return xc * lax.rsqrt(var + eps) * g + b


def _mla_in_kernel(x_ref, w_ref, qg_ref, kvg_ref, cos64_ref, sin64_ref, cos512_ref, sin512_ref,
                   wqn_ref, wqr_ref, wqrr_ref, wuk_ref,
                   ckv_ref, kr_ref, kcat_ref, qcat_ref):
    x = x_ref[0].astype(BF16)
    p = _nt(x, w_ref[...])
    cq = p[:, :Q_RANK]
    cq = cq * lax.rsqrt(jnp.mean(cq * cq, -1, keepdims=True) + 1e-6) * qg_ref[...]
    ckv = p[:, Q_RANK:Q_RANK + KV_RANK]
    ckv = ckv * lax.rsqrt(jnp.mean(ckv * ckv, -1, keepdims=True) + 1e-6) * kvg_ref[...]
    kr = p[:, 768:832] * cos64_ref[0] + p[:, 896:960] * sin64_ref[0]
    ckv_ref[0] = ckv
    kr_ref[0] = kr
    kcat_ref[0, :, :KV_RANK] = ckv.astype(BF16)
    kcat_ref[0, :, KV_RANK:] = kr.astype(BF16)
    cqb = cq.astype(BF16)
    qn = _nt(cqb, wqn_ref[...])
    qr = _nt(cqb, wqr_ref[...])
    qrr = _nt(cqb, wqrr_ref[...])
    qrope = (qr * cos512_ref[0] + qrr * sin512_ref[0]) * Q_PRESCALE
    for h in range(MLA_HEADS):
        ql = _nt(qn[:, h * QK_NOPE:(h + 1) * QK_NOPE].astype(BF16), wuk_ref[h])
        qcat_ref[0, h, :, :KV_RANK] = (ql * Q_PRESCALE).astype(BF16)
        qcat_ref[0, h, :, KV_RANK:] = qrope[:, h * QK_ROPE:(h + 1) * QK_ROPE].astype(BF16)


def _mla_in(x, tabs, lw):
    B, T, D = x.shape
    tm = _tile(T, 512)
    cos64, sin64, cos512, sin512 = tabs
    row = lambda w: pl.BlockSpec((1, tm, w), lambda b, t: (b, t, 0))
    tab = lambda w: pl.BlockSpec((1, tm, w), lambda b, t: (0, t, 0))
    full = lambda a: _resident(a, lw['layer'])
    hd = lambda w: pl.BlockSpec((1, MLA_HEADS, tm, w), lambda b, t: (b, 0, t, 0))
    ws = (lw['w_mla_t'], lw['q_norm_g'], lw['kv_norm_g'])
    ws2 = (lw['w_uq_nope_t'], lw['w_uq_rope_t'], lw['w_uq_rope_rot_t'], lw['w_uk'])
    return pl.pallas_call(
        _mla_in_kernel,
        grid=(B, T // tm),
        in_specs=[row(D)] + [full(a) for a in ws] + [tab(64), tab(64), tab(512), tab(512)] + [full(a) for a in ws2],
        out_specs=[row(KV_RANK), row(QK_ROPE), row(QK_CAT), hd(QK_CAT)],
        out_shape=[jax.ShapeDtypeStruct((B, T, KV_RANK), F32), jax.ShapeDtypeStruct((B, T, QK_ROPE), F32),
                   jax.ShapeDtypeStruct((B, T, QK_CAT), BF16),
                   jax.ShapeDtypeStruct((B, MLA_HEADS, T, QK_CAT), BF16)],
        compiler_params=_cparams(("parallel", "parallel")),
        name="mla_in",
    )(x, *ws, cos64, sin64, cos512, sin512, *ws2)


ATTN_HEAD_GROUP = 2


def _attn_kernel(qi_ref, kj_ref, q_ref, k_ref, wuv_ref, o_ref, m_sc, l_sc, acc_sc, *, tq, tk):
    step = pl.program_id(1)
    qi = qi_ref[step]
    kj = kj_ref[step]
    H = MLA_HEADS
    hg = ATTN_HEAD_GROUP
    n_groups = H // hg
    R = hg * tq

    @pl.when(kj == 0)
    def _():
        m_sc[...] = jnp.full(m_sc.shape, -jnp.inf, F32)
        l_sc[...] = jnp.zeros(l_sc.shape, F32)
        acc_sc[...] = jnp.zeros(acc_sc.shape, F32)

    def update(masked):
        k = k_ref[0]
        kc = k_ref[0, :, :KV_RANK]
        if masked:
            q_pos = qi * tq + lax.broadcasted_iota(jnp.int32, (tq, tk), 0)
            k_pos = kj * tk + lax.broadcasted_iota(jnp.int32, (tq, tk), 1)
            visible = (k_pos <= q_pos)[None]

        def scores(g):
            return _nt(q_ref[0, g * hg:(g + 1) * hg].reshape(R, QK_CAT), k)

        def softmax(g, s):
            rows = slice(g * R, (g + 1) * R)
            if masked:
                s = jnp.where(visible, s.reshape(hg, tq, tk), -jnp.inf).reshape(R, tk)
            m_old = m_sc[rows]
            m_new = jnp.maximum(m_old, jnp.max(s, -1, keepdims=True))
            scale = jnp.exp2(m_old - m_new)
            p = jnp.exp2(s - _repeat(m_new, tk // LANES, axis=1))
            l_sc[rows] = scale * l_sc[rows] + jnp.sum(p, -1, keepdims=True)
            m_sc[rows] = m_new
            return p.astype(BF16), scale

        def values(g, p, scale):
            rows = slice(g * R, (g + 1) * R)
            acc_sc[rows] = (_repeat(scale, KV_RANK // LANES, axis=1) * acc_sc[rows]
                            + jnp.dot(p, kc, preferred_element_type=F32))

        s_next = scores(0)
        pending = None
        for g in range(n_groups):
            s_cur = s_next
            if g + 1 < n_groups:
                s_next = scores(g + 1)
            if pending is not None:
                values(*pending)
            pending = (g,) + softmax(g, s_cur)
        values(*pending)

    needs_mask = kj * tk + tk - 1 > qi * tq
    pl.when(needs_mask)(functools.partial(update, True))
    pl.when(jnp.logical_not(needs_mask))(functools.partial(update, False))

    @pl.when(kj == (qi * tq + tq - 1) // tk)
    def _():
        o = (acc_sc[...] / _repeat(l_sc[...], KV_RANK // LANES, axis=1)).astype(BF16)
        for h in range(H):
            o_ref[0, :, h * V_HEAD:(h + 1) * V_HEAD] = jnp.dot(
                o[h * tq:(h + 1) * tq], wuv_ref[h], preferred_element_type=F32).astype(o_ref.dtype)


def _attn_prompt(qcat, kcat, lw):
    B, H, T, _ = qcat.shape
    w_uv = lw['w_uv']
    tq = _tile(T, 256)
    tk = _tile(T, 512)
    pairs = [(i, j) for i in range(T // tq) for j in range((i * tq + tq - 1) // tk + 1)]
    qi_of = jnp.asarray([p[0] for p in pairs], jnp.int32)
    kj_of = jnp.asarray([p[1] for p in pairs], jnp.int32)
    grid_spec = pltpu.PrefetchScalarGridSpec(
        num_scalar_prefetch=2,
        grid=(B, len(pairs)),
        in_specs=[pl.BlockSpec((1, H, tq, QK_CAT), lambda b, p, qi, kj: (b, 0, qi[p], 0)),
                  pl.BlockSpec((1, tk, QK_CAT), lambda b, p, qi, kj: (b, kj[p], 0)),
                  _resident(w_uv, lw['layer'])],
        out_specs=pl.BlockSpec((1, tq, MLA_WIDTH), lambda b, p, qi, kj: (b, qi[p], 0)),
        scratch_shapes=[pltpu.VMEM((H * tq, LANES), F32), pltpu.VMEM((H * tq, LANES), F32),
                        pltpu.VMEM((H * tq, KV_RANK), F32)])
    return pl.pallas_call(
        functools.partial(_attn_kernel, tq=tq, tk=tk),
        grid_spec=grid_spec,
        out_shape=jax.ShapeDtypeStruct((B, T, MLA_WIDTH), BF16),
        compiler_params=_cparams(("parallel", "arbitrary")),
        name="attn_prompt",
    )(qi_of, kj_of, qcat, kcat, w_uv)


ATTN_SLOTS = 3


def _attn_sample_kernel(pt_ref, q_ref, cnew_ref, rnew_ref, wuv_ref, cc_hbm, cr_hbm, o_ref,
                        cbuf, rbuf, sem, s_sc, kb_sc, *, layer, n_pages, chunk):
    b = pl.program_id(0)
    nb = pl.num_programs(0)
    H = MLA_HEADS
    n_keys = n_pages * PAGE_SIZE

    n_chunks = n_keys // chunk
    pages_per_chunk = n_pages // n_chunks

    def copies(bb, slot, pages=range(n_pages)):
        out = []
        for j in pages:
            pg = pt_ref[bb, j]
            keys = pl.ds(j * PAGE_SIZE, PAGE_SIZE)
            out.append(pltpu.make_async_copy(cc_hbm.at[layer, pg], cbuf.at[slot, keys], sem.at[0, slot]))
            out.append(pltpu.make_async_copy(cr_hbm.at[layer, pg], rbuf.at[slot, j], sem.at[1, slot]))
        return out

    @pl.when(b == 0)
    def _():
        for c in copies(0, 0) + copies(1, 1):
            c.start()

    slot = lax.rem(b, ATTN_SLOTS)
    slot_ahead = lax.rem(b + 2, ATTN_SLOTS)
    seq_ahead = jnp.minimum(b + 2, nb - 1)

    for c in copies(b, slot):
        c.wait()

    ql = q_ref[0, :, :KV_RANK]
    qr = q_ref[0, :, KV_RANK:]
    for c in range(n_chunks):
        keys = slice(c * chunk, (c + 1) * chunk)
        kc = cbuf[slot, keys].astype(BF16)
        kb_sc[keys] = kc
        pages = range(c * pages_per_chunk, (c + 1) * pages_per_chunk)
        kr_t = jnp.concatenate([rbuf[slot, j] for j in pages], axis=1).astype(BF16)
        s_sc[:, keys] = _nt(ql, kc) + jnp.dot(qr, kr_t, preferred_element_type=F32)
        for cp in copies(seq_ahead, slot_ahead, range(c * pages_per_chunk, (c + 1) * pages_per_chunk)):
            cp.start()
    cn = cnew_ref[0].astype(BF16)
    s_new = (jnp.sum(ql.astype(F32) * cn.astype(F32), -1, keepdims=True)
             + jnp.sum(qr.astype(F32) * rnew_ref[0].astype(BF16).astype(F32), -1, keepdims=True))
    s = s_sc[...]
    m = jnp.maximum(jnp.max(s, -1, keepdims=True), s_new)
    p = jnp.exp2(s - m)
    p_new = jnp.exp2(s_new - m)
    l = jnp.sum(p, -1, keepdims=True) + p_new
    p = (p / l).astype(BF16)
    parts = [(p_new / l).astype(BF16).astype(F32) * cn.astype(F32)]
    for c in range(n_chunks):
        keys = slice(c * chunk, (c + 1) * chunk)
        parts.append(jnp.dot(p[:, keys], kb_sc[keys], preferred_element_type=F32))
    while len(parts) > 1:
        parts = [a + b for a, b in zip(parts[::2], parts[1::2])] + parts[len(parts) - len(parts) % 2:]
    ob = parts[0].astype(BF16)
    for h in range(H):
        o_ref[0, :, h * V_HEAD:(h + 1) * V_HEAD] = jnp.dot(
            ob[h:h + 1], wuv_ref[h], preferred_element_type=F32).astype(o_ref.dtype)

    @pl.when(b == nb - 1)
    def _():
        for c in copies(b, lax.rem(b + 1, ATTN_SLOTS)) + copies(b, slot_ahead):
            c.wait()


def _attn_sample(page_table, qcat, ckv_new, kr_new, lw, cache_c, cache_rt):
    B, n_pages = page_table.shape
    H = MLA_HEADS
    n_keys = n_pages * PAGE_SIZE
    chunk = _tile(n_keys, 1024)
    layer, w_uv = lw['layer'], lw['w_uv']
    assert B >= 2 and n_keys % chunk == 0 and chunk % PAGE_SIZE == 0
    grid_spec = pltpu.PrefetchScalarGridSpec(
        num_scalar_prefetch=1,
        grid=(B,),
        in_specs=[pl.BlockSpec((1, H, QK_CAT), lambda b, pt: (b, 0, 0)),
                  pl.BlockSpec((1, 1, KV_RANK), lambda b, pt: (b, 0, 0)),
                  pl.BlockSpec((1, 1, QK_ROPE), lambda b, pt: (b, 0, 0)),
                  _resident(w_uv, layer),
                  pl.BlockSpec(memory_space=pl.ANY),
                  pl.BlockSpec(memory_space=pl.ANY)],
        out_specs=pl.BlockSpec((1, 1, MLA_WIDTH), lambda b, pt: (b, 0, 0)),
        scratch_shapes=[pltpu.VMEM((ATTN_SLOTS, n_keys, KV_RANK), F32),
                        pltpu.VMEM((ATTN_SLOTS, n_pages, QK_ROPE, PAGE_SIZE), F32),
                        pltpu.SemaphoreType.DMA((2, ATTN_SLOTS)),
                        pltpu.VMEM((H, n_keys), F32),
                        pltpu.VMEM((n_keys, KV_RANK), BF16)])
    return pl.pallas_call(
        functools.partial(_attn_sample_kernel, layer=layer, n_pages=n_pages, chunk=chunk),
        grid_spec=grid_spec,
        out_shape=jax.ShapeDtypeStruct((B, 1, MLA_WIDTH), BF16),
        compiler_params=_cparams(("arbitrary",)),
        name="attn_sample",
    )(page_table, qcat, ckv_new, kr_new, w_uv, cache_c, cache_rt)


def _gelu_tanh(x):
    return 0.5 * x * (1.0 + jnp.tanh(0.7978845608028654 * (x + 0.044715 * (x * x * x))))


def _gm_kernel(x_ref, w_ref, lng_ref, lnb_ref, ws_ref, bs_ref, y_ref, *v_out, tm, single):
    x = x_ref[...].astype(BF16)
    z = _gelu_tanh(_nt(x, w_ref[...]))
    u = z[:, :GM_WIDTH]
    v = _layer_norm_rows(z[:, GM_WIDTH:], lng_ref[...], lnb_ref[...], 1e-5)
    if single:
        y_ref[...] = (u * (v * ws_ref[...] + bs_ref[...])).astype(y_ref.dtype)
        v_out[0][...] = v
        return
    t_i = lax.broadcasted_iota(jnp.int32, (GM_CHUNK, GM_CHUNK), 0)
    s_i = lax.broadcasted_iota(jnp.int32, (GM_CHUNK, GM_CHUNK), 1)
    wm = [jnp.where(s_i <= t_i, ws_ref[g], 0.0).astype(BF16) for g in range(GM_GROUPS)]
    vb = v.astype(BF16)
    for c in range(tm // GM_CHUNK):
        rows = slice(c * GM_CHUNK, (c + 1) * GM_CHUNK)
        for g in range(GM_GROUPS):
            cols = slice(g * GM_GROUP_W, (g + 1) * GM_GROUP_W)
            mixed = jnp.dot(wm[g], vb[rows, cols], preferred_element_type=F32) + bs_ref[:, cols]
            y_ref[rows, cols] = (u[rows, cols] * mixed).astype(y_ref.dtype)


def _gmlp(x2, lw, single):
    M, D = x2.shape
    tm = _tile(M, 512)
    ws, bs = (lw['gm_w00'], lw['gm_b0']) if single else (lw['gm_w_s'], lw['gm_b_full'])
    full = lambda a: _resident(a, lw['layer'])
    row = lambda w: pl.BlockSpec((tm, w), lambda i: (i, 0))
    out_specs = [row(GM_WIDTH)]
    out_shape = [jax.ShapeDtypeStruct((M, GM_WIDTH), BF16)]
    if single:
        out_specs.append(row(GM_WIDTH))
        out_shape.append(jax.ShapeDtypeStruct((M, GM_WIDTH), F32))
    ins = (lw['w_gm_t'], lw['gm_ln_g'], lw['gm_ln_b'], ws, bs)
    return pl.pallas_call(
        functools.partial(_gm_kernel, tm=tm, single=single),
        grid=(M // tm,),
        in_specs=[row(D)] + [full(a) for a in ins],
        out_specs=out_specs, out_shape=out_shape,
        compiler_params=_cparams(("parallel",)),
        name="gmlp",
    )(x2, *ins)


def _rw_in_kernel(x_ref, prev_ref, w_ref, mu_ref, wda_ref, w0a0_ref, g2_ref, kk_ref_w, ka_ref, rk_ref, e_ref,
                  *refs, tm, seq):
    outs, (g_o, bonus_o, shift_o, carry) = refs[:6], refs[6:]
    t = pl.program_id(1)
    x = x_ref[0].astype(BF16)
    p = _nt(x, w_ref[...])
    if seq:
        @pl.when(t == 0)
        def _():
            carry[...] = prev_ref[0]
        row = lax.broadcasted_iota(jnp.int32, (tm, 1), 0)
        prev = jnp.where(row == 0, carry[...], pltpu.roll(p, 1, axis=0))
        carry[...] = p[tm - 1:tm]
    else:
        prev = prev_ref[0]
    shift_o[0] = p[tm - 1:tm] if seq else p
    xm = p + (prev - p) * mu_ref[...]
    r = xm[:, :R_WIDTH]
    k = xm[:, R_WIDTH:2 * R_WIDTH]
    v = xm[:, 2 * R_WIDTH:3 * R_WIDTH]
    da = xm[:, 3 * R_WIDTH:3 * R_WIDTH + D_DECAY + D_AAA]
    lane = lax.broadcasted_iota(jnp.int32, da.shape, 1)
    da = jnp.where(lane < D_DECAY, jnp.tanh(da), da)
    m = jnp.dot(da.astype(BF16), wda_ref[...], preferred_element_type=F32) + w0a0_ref[...]
    wv = m[:, :R_WIDTH]
    w_log = -(jnp.maximum(-wv, 0.0) + jnp.log1p(jnp.exp(-jnp.abs(wv)))) - 0.5
    logdecay = -jnp.exp(w_log)
    a = jax.nn.sigmoid(m[:, R_WIDTH:])
    gd = xm[:, 3 * R_WIDTH + D_DECAY + D_AAA:]
    g = jnp.dot(jax.nn.sigmoid(gd).astype(BF16), g2_ref[...], preferred_element_type=F32)
    e = e_ref[...]
    kk = k * kk_ref_w[...]
    kk = kk * lax.rsqrt(jnp.maximum(_head_sums(kk * kk, e), 1e-24))
    k2 = k * (1.0 + (a - 1.0) * ka_ref[...])
    bonus = _head_sums(r * k2 * rk_ref[...], e) * v
    g_o[0] = g
    bonus_o[0] = bonus
    bb = kk * a
    if seq:
        C = WKV_CHUNK
        ti = lax.broadcasted_iota(jnp.int32, (C, C), 0)
        si = lax.broadcasted_iota(jnp.int32, (C, C), 1)
        tri = jnp.where(si <= ti, 1.0, 0.0).astype(BF16)
        ci = lax.broadcasted_iota(jnp.int32, (tm // C, tm), 0)
        sj = lax.broadcasted_iota(jnp.int32, (tm // C, tm), 1)
        in_chunk = jnp.where(sj // C == ci, 1.0, 0.0).astype(BF16)
        cum = jnp.concatenate([_sel_dot_left(tri, logdecay[c * C:(c + 1) * C]) for c in range(tm // C)], axis=0)
        p_in = jnp.exp(cum)
        p_inv = jnp.exp(-cum)
        pc_o = outs[5]
        pc_o[0] = jnp.exp(_sel_dot_left(in_chunk, logdecay))
        vals = (-kk * jnp.exp(cum - logdecay), bb * p_inv, k2 * p_inv, r * p_in, v)
        for h in range(R_HEADS):
            c = slice(h * R_HEAD_DIM, (h + 1) * R_HEAD_DIM)
            for o_ref, val in zip(outs, vals):
                o_ref[0, h] = val[:, c].astype(o_ref.dtype)
    else:
        for o_ref, val in zip(outs, (r, logdecay, k2, v, kk, bb)):
            o_ref[0] = val


def _rw_in(x, prev, lw, seq):
    B, T, D = x.shape
    tm = _tile(T, 512 if seq else 256)
    row = lambda w: pl.BlockSpec((1, tm, w), lambda b, t: (b, t, 0))
    hd = pl.BlockSpec((1, R_HEADS, tm, R_HEAD_DIM), lambda b, t: (b, 0, t, 0))
    ws = (lw['w_rw_t'], lw['rw_mu'], lw['rw_wda'], lw['rw_w0a0'], lw['rw_g2'], lw['rw_k_k'], lw['rw_k_a'],
          lw['rw_r_k'])
    w_specs = [_resident(a, lw['layer']) for a in ws] + [_resident(lw['head_ind'])]
    ws = ws + (lw['head_ind'],)
    hshape = lambda dt: jax.ShapeDtypeStruct((B, R_HEADS, T, R_HEAD_DIM), dt)
    if seq:
        C = WKV_CHUNK
        prev_spec = pl.BlockSpec((1, 1, R_SHIFT_W), lambda b, t: (b, 0, 0))
        shift_spec = pl.BlockSpec((1, 1, R_SHIFT_W), lambda b, t: (b, 0, 0))
        shift_shape = jax.ShapeDtypeStruct((B, 1, R_SHIFT_W), F32)
        head_specs = [hd] * 5 + [pl.BlockSpec((1, tm // C, R_WIDTH), lambda b, t: (b, t, 0))]
        head_shapes = [hshape(BF16)] * 5 + [jax.ShapeDtypeStruct((B, T // C, R_WIDTH), F32)]
    else:
        prev_spec = row(R_SHIFT_W)
        shift_spec = row(R_SHIFT_W)
        shift_shape = jax.ShapeDtypeStruct((B, T, R_SHIFT_W), F32)
        head_specs = [row(R_WIDTH)] * 6
        head_shapes = [jax.ShapeDtypeStruct((B, T, R_WIDTH), F32)] * 6
    fshape = jax.ShapeDtypeStruct((B, T, R_WIDTH), F32)
    return pl.pallas_call(
        functools.partial(_rw_in_kernel, tm=tm, seq=seq),
        grid=(B, T // tm),
        in_specs=[row(D), prev_spec] + w_specs,
        out_specs=head_specs + [row(R_WIDTH), row(R_WIDTH), shift_spec],
        out_shape=head_shapes + [fshape, fshape, shift_shape],
        scratch_shapes=[pltpu.VMEM((1, R_SHIFT_W), F32)],
        compiler_params=_cparams(("parallel", "arbitrary")),
        name="rw_in",
    )(x, prev, *ws)


def _bdot(a, b):
    return jnp.dot(a.astype(BF16), b.astype(BF16), preferred_element_type=F32)


def _bdot_nt(a, b):
    return _nt(a.astype(BF16), b.astype(BF16))


def _bdot_tn(a, b):
    return _tn(a.astype(BF16), b.astype(BF16))


def _wkv_kernel(a_ref, b_ref, k_ref, r_ref, v_ref, pc_ref, y_ref, s_ref, state):
    c = pl.program_id(0)
    C = WKV_CHUNK
    n_seq, n_head, rows, _ = a_ref.shape
    n_sub = rows // C

    @pl.when(c == 0)
    def _():
        state[...] = jnp.zeros(state.shape, F32)

    ti = lax.broadcasted_iota(jnp.int32, (C, 2 * C), 0)
    si = lax.broadcasted_iota(jnp.int32, (C, 2 * C), 1) % C
    incl2 = si <= ti
    strict2 = si < ti
    n_steps = C.bit_length() - 1
    ch = [(b, h, u) for u in range(n_sub) for b in range(n_seq) for h in range(n_head)]
    each = lambda f, *lists: [f(*args) for args in zip(*lists)]
    A = [a_ref[b, h, u * C:(u + 1) * C] for b, h, u in ch]
    Bm = [b_ref[b, h, u * C:(u + 1) * C] for b, h, u in ch]
    K = [k_ref[b, h, u * C:(u + 1) * C] for b, h, u in ch]
    R = [r_ref[b, h, u * C:(u + 1) * C] for b, h, u in ch]
    V = [v_ref[b, h, u * C:(u + 1) * C] for b, h, u in ch]
    D = R_HEAD_DIM
    BK = each(lambda bm, k: jnp.concatenate([bm, k], axis=0), Bm, K)
    NM = each(lambda a, bk: jnp.where(strict2, _nt(a, bk), 0.0), A, BK)
    Nb = [nm[:, :C].astype(BF16) for nm in NM]
    XU = each(lambda nm, v: _bdot(nm[:, C:], v), NM, V)
    X = each(lambda a, xu: jnp.concatenate([a.astype(F32), xu], axis=1), A, XU)
    for i in range(n_steps):
        X = each(lambda n, x: x + _bdot(n, x), Nb, X)
        if i + 1 < n_steps:
            Nb = each(lambda n: _bdot(n, n).astype(BF16), Nb)
    RBK = each(lambda r, bk: jnp.where(incl2, _nt(r, bk), 0.0).astype(BF16), R, BK)
    RBX = each(lambda rbk, x: _bdot(rbk[:, :C], x), RBK, X)
    Q = each(lambda r, rbx: r.astype(F32) + rbx[:, :D], R, RBX)
    Y0 = each(lambda rbx, rbk, v: rbx[:, D:] + _bdot(rbk[:, C:], v), RBX, RBK, V)
    GH = each(_bdot_tn, X, Bm)
    G = [gh[:D] for gh in GH]
    Hm = each(lambda gh, v, k: gh[D:] + _bdot_tn(v, k), GH, V, K)
    heads = [(b, h) for b in range(n_seq) for h in range(n_head)]
    n_ch = len(heads)
    S = [state[b, h] for b, h in heads]
    for u in range(n_sub):
        part = slice(u * n_ch, (u + 1) * n_ch)
        Y = each(lambda q, s, y0: _bdot_nt(q, s) + y0, Q[part], S, Y0[part])
        pc = [pc_ref[b, 0, u:u + 1, h * D:(h + 1) * D] for b, h in heads]
        S = each(lambda s, g, hm, p: (s + _bdot(s, g) + hm) * p, S, G[part], Hm[part], pc)
        for b in range(n_seq):
            y_ref[b, u * C:(u + 1) * C] = jnp.concatenate(Y[b * n_head:(b + 1) * n_head], axis=-1)
    for (b, h), s in zip(heads, S):
        state[b, h] = s
        s_ref[b, h] = s


def _wkv_prompt(A, Bm, K, R, V, pc):
    B, H, T, N = A.shape
    C = WKV_CHUNK
    n_sub = WKV_CHUNKS_PER_STEP if T % (C * WKV_CHUNKS_PER_STEP) == 0 else 1
    rows = n_sub * C
    hd = pl.BlockSpec((B, H, rows, N), lambda c: (0, 0, c, 0))
    return pl.pallas_call(
        _wkv_kernel,
        grid=(T // rows,),
        in_specs=[hd] * 5 + [pl.BlockSpec((B, 1, n_sub, H * N), lambda c: (0, c, 0, 0))],
        out_specs=[pl.BlockSpec((B, rows, H * N), lambda c: (0, c, 0)),
                   pl.BlockSpec((B, H, N, N), lambda c: (0, 0, 0, 0))],
        out_shape=[jax.ShapeDtypeStruct((B, T, H * N), F32), jax.ShapeDtypeStruct((B, H, N, N), F32)],
        scratch_shapes=[pltpu.VMEM((B, H, N, N), F32)],
        compiler_params=_cparams(("arbitrary",)),
        name="wkv_prompt",
    )(A, Bm, K, R, V, pc.reshape(B, T // rows, n_sub, H * N))


def _wkv_step_kernel(r_ref, lw_ref, k_ref, v_ref, kk_ref, b_ref, s_ref, sel_ref, rep_ref, y_ref, so_ref):
    n = R_HEAD_DIM
    h = pl.program_id(0)
    sel = sel_ref[...]
    rep = rep_ref[...]
    head_t = lambda ref: _exact_apply(lambda p: _nt(sel, p), ref[0])
    over_i = lambda x: _repeat(x, n, axis=0)
    over_j = lambda x: _sel_dot_left(rep, x)
    sum_j = lambda x: _exact_apply(lambda p: _tn(rep, p), x)
    S = s_ref[...].reshape(n * n, s_ref.shape[-1])
    sa = sum_j(S * over_i(-head_t(kk_ref)))
    S1 = (S * over_i(jnp.exp(head_t(lw_ref))) + over_j(sa) * over_i(head_t(b_ref))
          + over_j(head_t(v_ref)) * over_i(head_t(k_ref)))
    so_ref[...] = S1.reshape(so_ref.shape)
    y_t = sum_j(S1 * over_i(head_t(r_ref)))

    @pl.when(h == 0)
    def _():
        y_ref[...] = jnp.zeros(y_ref.shape, F32)

    y_ref[...] += _exact_apply(lambda p: _tn(p, sel), y_t)


def _wkv_step(r, lw, k, v, kk, b, S_all, layer, consts):
    _, Bd, W = r.shape
    _, H, n, _, _ = S_all.shape
    sel, rep = consts
    vec = pl.BlockSpec((1, Bd, W), lambda h: (0, 0, 0))
    return pl.pallas_call(
        _wkv_step_kernel,
        grid=(H,),
        in_specs=[vec] * 6 + [pl.BlockSpec((None, None, n, n, Bd), lambda h: (layer, h, 0, 0, 0)),
                              pl.BlockSpec((n, W), lambda h: (h, 0)), _resident(rep)],
        out_specs=[pl.BlockSpec((Bd, W), lambda h: (0, 0)),
                   pl.BlockSpec((None, n, n, Bd), lambda h: (h, 0, 0, 0))],
        out_shape=[jax.ShapeDtypeStruct((Bd, W), F32), jax.ShapeDtypeStruct((H, n, n, Bd), F32)],
        compiler_params=_cparams(("arbitrary",)),
        name="wkv_step",
    )(r, lw, k, v, kk, b, S_all, sel, rep)


def _rw_post_kernel(y_ref, bonus_ref, g_ref, lng_ref, lnb_ref, e_ref, o_ref):
    e = e_ref[...]
    y = y_ref[...]
    mean = _head_sums(y, e) * (1.0 / R_HEAD_DIM)
    yc = y - mean
    var = _head_sums(yc * yc, e) * (1.0 / R_HEAD_DIM)
    yn = yc * lax.rsqrt(var + GN_EPS) * lng_ref[...] + lnb_ref[...]
    o_ref[...] = ((yn + bonus_ref[...]) * g_ref[...]).astype(o_ref.dtype)


def _rw_post(y, bonus, g, lw):
    M, W = y.shape
    tm = _tile(M, 512)
    row = pl.BlockSpec((tm, W), lambda i: (i, 0))
    ws = (lw['rw_ln_g'], lw['rw_ln_b'], lw['head_ind'])
    w_specs = [_resident(lw['rw_ln_g'], lw['layer']), _resident(lw['rw_ln_b'], lw['layer']),
               _resident(lw['head_ind'])]
    return pl.pallas_call(
        _rw_post_kernel,
        grid=(M // tm,),
        in_specs=[row, row, row] + w_specs,
        out_specs=row,
        out_shape=jax.ShapeDtypeStruct((M, W), BF16),
        compiler_params=_cparams(("parallel",)),
        name="rw_post",
    )(y, bonus, g, *ws)


def _out_ln_kernel(x_ref, ya_ref, yg_ref, yr_ref, w_ref, g_ref, b_ref, o_ref, ob_ref):
    h = jnp.dot(ya_ref[...], w_ref[:MLA_WIDTH], preferred_element_type=F32)
    h = h + jnp.dot(yg_ref[...], w_ref[MLA_WIDTH:MLA_WIDTH + GM_WIDTH], preferred_element_type=F32)
    h = h + jnp.dot(yr_ref[...], w_ref[MLA_WIDTH + GM_WIDTH:], preferred_element_type=F32)
    y = _layer_norm_rows(ALPHA * x_ref[...] + h, g_ref[...], b_ref[...], 1e-5)
    o_ref[...] = y
    ob_ref[...] = y.astype(BF16)


def _out_ln(x2, y_mla, y_gm, y_rw, lw):
    M, D = x2.shape
    tm = _tile(M, 512)
    row = lambda w: pl.BlockSpec((tm, w), lambda i: (i, 0))
    full = lambda a: _resident(a, lw['layer'])
    ws = (lw['w_out'], lw['ln1_g'], lw['ln1_b'])
    return pl.pallas_call(
        _out_ln_kernel,
        grid=(M // tm,),
        in_specs=[row(D), row(MLA_WIDTH), row(GM_WIDTH), row(R_WIDTH)] + [full(a) for a in ws],
        out_specs=[row(D), row(D)],
        out_shape=[jax.ShapeDtypeStruct((M, D), F32), jax.ShapeDtypeStruct((M, D), BF16)],
        compiler_params=_cparams(("parallel",)),
        name="out_ln",
    )(x2, y_mla, y_gm, y_rw, *ws)


def _ffn_up_kernel(xb_ref, wg_ref, wu_ref, h_ref):
    xb = xb_ref[...]
    gate = jnp.dot(xb, wg_ref[...], preferred_element_type=F32)
    up = jnp.dot(xb, wu_ref[...], preferred_element_type=F32)
    h_ref[...] = (gate * jax.nn.sigmoid(gate) * up).astype(h_ref.dtype)


def _ffn_down_kernel(x_ref, h_ref, wo_ref, g_ref, b_ref, o_ref, ob_ref):
    y = jnp.dot(h_ref[...], wo_ref[...], preferred_element_type=F32)
    y = _layer_norm_rows(ALPHA * x_ref[...] + y, g_ref[...], b_ref[...], 1e-5)
    o_ref[...] = y
    ob_ref[...] = y.astype(BF16)


def _ffn(x2, xb2, lw):
    M, D = x2.shape
    l = lw['layer']
    tm = _tile(M, 1024)
    tf = 512
    nf = D_FF // tf
    hid = pl.pallas_call(
        _ffn_up_kernel,
        grid=(M // tm, nf),
        in_specs=[pl.BlockSpec((tm, D), lambda i, j: (i, 0)),
                  pl.BlockSpec((None, D, tf), lambda i, j: (l, 0, j)),
                  pl.BlockSpec((None, D, tf), lambda i, j: (l, 0, j + nf))],
        out_specs=pl.BlockSpec((tm, tf), lambda i, j: (i, j)),
        out_shape=jax.ShapeDtypeStruct((M, D_FF), BF16),
        compiler_params=_cparams(("parallel", "arbitrary")),
        name="ffn_up",
    )(xb2, lw['w_ffn_in'], lw['w_ffn_in'])
    tm = _tile(M, 512)
    w_out = lw['w_ffn_out']
    return pl.pallas_call(
        _ffn_down_kernel,
        grid=(M // tm,),
        in_specs=[pl.BlockSpec((tm, D), lambda i: (i, 0)),
                  pl.BlockSpec((tm, D_FF), lambda i: (i, 0)),
                  pl.BlockSpec((None,) + w_out.shape[1:], lambda i: (l, 0, 0), pipeline_mode=pl.Buffered(1)),
                  _resident(lw['ln2_g'], l), _resident(lw['ln2_b'], l)],
        out_specs=[pl.BlockSpec((tm, D), lambda i: (i, 0))] * 2,
        out_shape=[jax.ShapeDtypeStruct((M, D), F32), jax.ShapeDtypeStruct((M, D), BF16)],
        compiler_params=_cparams(("parallel",), V7X_VMEM_LIMIT_MAX),
        name="ffn_down",
    )(x2, hid, w_out, lw['ln2_g'], lw['ln2_b'])


def _rot_half_rows(w):
    half = QK_ROPE // 2
    return jnp.concatenate([-w[..., half:, :], w[..., :half, :]], axis=-2)


def _rope_tables(pos):
    half = QK_ROPE // 2
    inv = ROPE_THETA ** (-jnp.arange(half, dtype=F32) / half)
    ang = pos.astype(F32)[:, None] * inv[None, :]
    cos = jnp.cos(ang)
    sin = jnp.sin(ang)
    cos64 = jnp.concatenate([cos, cos], -1)[None]
    sin64 = jnp.concatenate([sin, sin], -1)[None]
    return cos64, sin64, jnp.tile(cos64, (1, 1, MLA_HEADS)), jnp.tile(sin64, (1, 1, MLA_HEADS))


def _prep_weights(P):
    L = P['w_in'].shape[0]
    w_in_t = jnp.swapaxes(P['w_in'], 1, 2)
    D = w_in_t.shape[2]
    o_gm = Q_RANK + KV_RANK + QK_ROPE
    o_rw = o_gm + 2 * GM_WIDTH
    w_kr_t = w_in_t[:, Q_RANK + KV_RANK:o_gm]
    z64 = jnp.zeros((L, 64, D), F32)
    w_mla_t = jnp.concatenate([w_in_t[:, :Q_RANK + KV_RANK], w_kr_t, z64, _rot_half_rows(w_kr_t), z64], axis=1)
    w_uq_t = jnp.transpose(P['w_uq'], (0, 2, 3, 1))
    w_uq_rope_t = w_uq_t[:, :, QK_NOPE:]
    zda = jnp.zeros((L, D_DECAY, R_WIDTH), F32)
    wda = jnp.concatenate([jnp.concatenate([P['rw_w2'], zda], 2),
                           jnp.concatenate([zda, P['rw_a2']], 2)], 1)
    hid = jnp.arange(R_WIDTH) // R_HEAD_DIM
    row = lambda a: a.reshape(L, 1, -1)
    gm_w = P['gm_w_s']
    gm_b = P['gm_b_s']
    return {
        'w_mla_t': w_mla_t.astype(BF16),
        'q_norm_g': row(P['q_norm_g']), 'kv_norm_g': row(P['kv_norm_g']),
        'w_uq_nope_t': w_uq_t[:, :, :QK_NOPE].reshape(L, -1, Q_RANK).astype(BF16),
        'w_uq_rope_t': w_uq_rope_t.reshape(L, -1, Q_RANK).astype(BF16),
        'w_uq_rope_rot_t': _rot_half_rows(w_uq_rope_t).reshape(L, -1, Q_RANK).astype(BF16),
        'w_uk': P['w_uk'].astype(BF16),
        'w_uv': P['w_uv'].astype(BF16),
        'w_gm_t': w_in_t[:, o_gm:o_rw].astype(BF16),
        'gm_ln_g': row(P['gm_ln_g']), 'gm_ln_b': row(P['gm_ln_b']),
        'gm_w_s': gm_w,
        'gm_b_full': jnp.repeat(jnp.swapaxes(gm_b, 1, 2), GM_GROUP_W, axis=2),
        'gm_w00': row(jnp.repeat(gm_w[:, :, 0, 0], GM_GROUP_W, axis=1)),
        'gm_b0': row(jnp.repeat(gm_b[:, :, 0], GM_GROUP_W, axis=1)),
        'w_rw_t': w_in_t[:, o_rw:].astype(BF16),
        'rw_mu': row(P['rw_mu']),
        'rw_wda': wda.astype(BF16),
        'rw_w0a0': row(jnp.concatenate([P['rw_w0'], P['rw_a0']], axis=1)),
        'rw_g2': P['rw_g2'].astype(BF16),
        'rw_k_k': row(P['rw_k_k']), 'rw_k_a': row(P['rw_k_a']), 'rw_r_k': row(P['rw_r_k']),
        'rw_ln_g': row(P['rw_ln_g']), 'rw_ln_b': row(P['rw_ln_b']),
        'head_ind': (hid[:, None] == hid[None, :]).astype(BF16),
        'w_out': P['w_out'].astype(BF16),
        'ln1_g': row(P['ln1_g']), 'ln1_b': row(P['ln1_b']),
        'w_ffn_in': P['w_ffn_in'].astype(BF16),
        'w_ffn_out': P['w_ffn_out'].astype(BF16),
        'ln2_g': row(P['ln2_g']), 'ln2_b': row(P['ln2_b']),
    }


def _step_consts():
    n = R_HEAD_DIM
    sel = jnp.eye(R_WIDTH, dtype=BF16)
    rep = (jnp.arange(n * n)[:, None] // n == jnp.arange(n)[None, :]).astype(BF16)
    return sel, rep


def _layer_prompt(x, xb, tabs, lw):
    B, T, D = x.shape
    ckv, kr, kcat, qcat = _mla_in(xb, tabs, lw)
    y_mla = _attn_prompt(qcat, kcat, lw)
    y_gm, = _gmlp(xb.reshape(B * T, D), lw, single=False)
    shift0 = jnp.zeros((B, 1, R_SHIFT_W), F32)
    A, Bm, K, R, V, pc, g, bonus, shift = _rw_in(xb, shift0, lw, seq=True)
    y, S = _wkv_prompt(A, Bm, K, R, V, pc)
    y_rw = _rw_post(y.reshape(B * T, R_WIDTH), bonus.reshape(B * T, R_WIDTH), g.reshape(B * T, R_WIDTH), lw)
    x2, xb2 = _out_ln(x.reshape(B * T, D), y_mla.reshape(B * T, MLA_WIDTH), y_gm, y_rw, lw)
    x2, xb2 = _ffn(x2, xb2, lw)
    return x2.reshape(B, T, D), xb2.reshape(B, T, D), ckv, kr, S, shift.reshape(B, R_SHIFT_W)


def _layer_sample(x, xb, tabs, lw, consts, page_table, cache_c, cache_r, shift0, S_all):
    _, Bd, D = x.shape
    ckv, kr, _, qcat = _mla_in(xb, tabs, lw)
    tr = lambda a: jnp.swapaxes(a[0], 0, 1)
    y_mla = _attn_sample(page_table, tr(qcat), ckv.reshape(Bd, 1, KV_RANK),
                         kr.reshape(Bd, 1, QK_ROPE), lw, cache_c, cache_r)
    y_gm, v_gm = _gmlp(xb.reshape(Bd, D), lw, single=True)
    r, lgw, k, v, kk, b, g, bonus, shift = _rw_in(xb, shift0[None], lw, seq=False)
    y, S = _wkv_step(r, lgw, k, v, kk, b, S_all, lw['layer'], consts)
    y_rw = _rw_post(y, bonus[0], g[0], lw)
    x2, xb2 = _out_ln(x.reshape(Bd, D), y_mla.reshape(Bd, MLA_WIDTH), y_gm, y_rw, lw)
    x2, xb2 = _ffn(x2, xb2, lw)
    return (x2.reshape(1, Bd, D), xb2.reshape(1, Bd, D), ckv.reshape(Bd, 1, KV_RANK), kr.reshape(Bd, 1, QK_ROPE),
            S, shift[0], v_gm.reshape(Bd, 1, GM_WIDTH))


def kernel(x_prompt, x_sample, cache_c_kv, cache_k_rope, state_rwkv_wkv, state_rwkv_shift, page_table, w_in, q_norm_g, kv_norm_g, w_uq, w_uk, w_uv, gm_ln_g, gm_ln_b, gm_w_s, gm_b_s, rw_mu, rw_w0, rw_w2, rw_a0, rw_a2, rw_g2, rw_k_k, rw_k_a, rw_r_k, rw_ln_g, rw_ln_b, w_out, ln1_g, ln1_b, w_ffn_in, w_ffn_out, ln2_g, ln2_b):
    P = dict(w_in=w_in, q_norm_g=q_norm_g, kv_norm_g=kv_norm_g, w_uq=w_uq, w_uk=w_uk, w_uv=w_uv,
             gm_ln_g=gm_ln_g, gm_ln_b=gm_ln_b, gm_w_s=gm_w_s, gm_b_s=gm_b_s, rw_mu=rw_mu, rw_w0=rw_w0,
             rw_w2=rw_w2, rw_a0=rw_a0, rw_a2=rw_a2, rw_g2=rw_g2, rw_k_k=rw_k_k, rw_k_a=rw_k_a,
             rw_r_k=rw_r_k, rw_ln_g=rw_ln_g, rw_ln_b=rw_ln_b, w_out=w_out, ln1_g=ln1_g, ln1_b=ln1_b,
             w_ffn_in=w_ffn_in, w_ffn_out=w_ffn_out, ln2_g=ln2_g, ln2_b=ln2_b)
    n_dec, t_dec, _ = x_sample.shape
    assert t_dec == 1
    t_prompt = x_prompt.shape[1]
    past_len = page_table.shape[1] * PAGE_SIZE
    tabs_p = _rope_tables(jnp.arange(t_prompt, dtype=jnp.int32))
    tabs_s = _rope_tables(jnp.full((n_dec,), past_len, dtype=jnp.int32))
    consts = _step_consts()
    cache_k_rope_t = jnp.swapaxes(cache_k_rope, 2, 3)
    wkv_t = jnp.transpose(state_rwkv_wkv, (0, 2, 3, 4, 1))
    xp = x_prompt
    xs = x_sample.reshape(1, n_dec, -1)
    xpb, xsb = xp.astype(BF16), xs.astype(BF16)
    outs = [[] for _ in range(9)]
    W = _prep_weights(P)
    for l in range(w_in.shape[0]):
        lw = dict(W, layer=l)
        xp, xpb, ckv_p, kr_p, S_p, sh_p = _layer_prompt(xp, xpb, tabs_p, lw)
        xs, xsb, ckv_s, kr_s, S_s, sh_s, vg_s = _layer_sample(xs, xsb, tabs_s, lw, consts, page_table, cache_c_kv,
                                                              cache_k_rope_t, state_rwkv_shift[l], wkv_t)
        for lst, a in zip(outs, (ckv_p, kr_p, ckv_s, kr_s, S_p, sh_p, S_s, sh_s, vg_s)):
            lst.append(a)
    outs = [jnp.stack(o) for o in outs]
    outs[6] = jnp.transpose(outs[6], (0, 4, 1, 2, 3))
    return (xp, xs.reshape(n_dec, 1, -1)) + tuple(outs)
```
